```python
import jax, jax.numpy as jnp
from jax import lax
import numpy as np

D_MODEL = 1024
BATCH = 8
SEQ = 4096
DEPTH = 1
DEC_BATCH = 128
DEC_SEQ = 4
PAST_LEN = 8192
PAGE_SIZE = 128

FOX_HEADS = 16
FOX_HEAD_DIM = D_MODEL // FOX_HEADS
W_FOX = FOX_HEADS * FOX_HEAD_DIM
RET_HEADS = 8
RET_DK = D_MODEL // RET_HEADS
RET_DV = D_MODEL // RET_HEADS
W_RET_K = RET_HEADS * RET_DK
W_RET_V = RET_HEADS * RET_DV
RET_CHUNK = 128
Q_BLOCK = 128
D_FF = -(-(8 * D_MODEL) // (3 * 256)) * 256
ROPE_BASE = 10000.0
NORM_EPS = 1e-6
GN_EPS = 1e-5
NEG = -1e30
FORGET_BIAS = 10.0
FORGET_W_SCALE = 0.1
SPLITS = (W_FOX, W_FOX, W_FOX, FOX_HEADS, W_RET_K, W_RET_K, W_RET_V, W_RET_V, D_MODEL, D_MODEL)
N_IN = sum(SPLITS)

kernel_name = 'fox_retention_gated_hybrid_step'

F32 = jnp.float32


def rms_norm(x, w):
    xf = x.astype(F32)
    y = xf * lax.rsqrt(jnp.mean(xf * xf, axis=-1, keepdims=True) + NORM_EPS)
    return (y * w.astype(F32)).astype(x.dtype)


def ada_terms(c, w_ada, b_ada):
    mod = jax.nn.silu(c) @ w_ada + b_ada
    return jnp.split(mod, 6, axis=-1)


def modulate(h, shift, scale):
    return (h * (1 + scale[:, None, :]) + shift[:, None, :]).astype(h.dtype)


def rope(x, pos):
    half = x.shape[-1] // 2
    inv = ROPE_BASE ** (-jnp.arange(half, dtype=F32) / half)
    ang = pos.astype(F32)[:, None] * inv[None, :]
    cos = jnp.cos(ang)[None, :, None, :]
    sin = jnp.sin(ang)[None, :, None, :]
    x1 = x[..., :half].astype(F32)
    x2 = x[..., half:].astype(F32)
    return jnp.concatenate([x1 * cos - x2 * sin, x1 * sin + x2 * cos], axis=-1).astype(x.dtype)


def retention_log_gamma():
    return jnp.log(1.0 - 2.0 ** (-5.0 - jnp.arange(RET_HEADS, dtype=F32)))


def mixer_inputs(h, w_in, b_f, pos):
    B, S = h.shape[0], h.shape[1]
    z = h @ w_in
    fq, fk, fv, f_logit, rq, rk, rv, rg, ga, gb = jnp.split(z, np.cumsum(SPLITS)[:-1].tolist(), axis=-1)
    fox_h = lambda t: t.reshape(B, S, FOX_HEADS, FOX_HEAD_DIM)
    fq, fk, fv = fox_h(fq), fox_h(fk), fox_h(fv)
    logf = jax.nn.log_sigmoid(f_logit.astype(F32) + b_f.astype(F32))
    rq = rope(rq.reshape(B, S, RET_HEADS, RET_DK), pos) * (RET_DK ** -0.5)
    rk = rope(rk.reshape(B, S, RET_HEADS, RET_DK), pos)
    rv = rv.reshape(B, S, RET_HEADS, RET_DV)
    return fq, fk, fv, logf, rq, rk, rv, rg, ga, gb


def fox_prompt(q, k, v, logf):
    B, S, H, Dh = q.shape
    scale = Dh ** -0.5
    F = jnp.cumsum(logf, axis=1).transpose(0, 2, 1)
    kpos = jnp.arange(S)

    def block(i):
        start = i * Q_BLOCK
        qb = lax.dynamic_slice_in_dim(q, start, Q_BLOCK, axis=1)
        Fq = lax.dynamic_slice_in_dim(F, start, Q_BLOCK, axis=2)
        s = jnp.einsum('bqhd,bkhd->bhqk', qb, k, preferred_element_type=F32) * scale
        s = s + Fq[..., :, None] - F[..., None, :]
        qpos = start + jnp.arange(Q_BLOCK)
        s = jnp.where(kpos[None, :] <= qpos[:, None], s, NEG)
        p = jax.nn.softmax(s, axis=-1)
        return jnp.einsum('bhqk,bkhd->bqhd', p.astype(v.dtype), v)

    out = lax.map(block, jnp.arange(S // Q_BLOCK))
    return out.transpose(1, 0, 2, 3, 4).reshape(B, S, H, Dh)


def fox_sample(q, k, v, logf, cache_k, cache_v, cache_logf, page_table):
    DB, T, H, Dh = q.shape
    n_pages = page_table.shape[1]
    ps = cache_k.shape[1]
    scale = Dh ** -0.5
    F_new = jnp.cumsum(logf, axis=1).transpose(0, 2, 1)
    past_logf = cache_logf[page_table].reshape(DB, n_pages * ps, H).astype(F32)
    suffix = jnp.flip(jnp.cumsum(jnp.flip(past_logf, axis=1), axis=1), axis=1) - past_logf
    s_new = jnp.einsum('bqhd,bkhd->bhqk', q, k, preferred_element_type=F32) * scale
    s_new = s_new + F_new[..., :, None] - F_new[..., None, :]
    causal = jnp.arange(T)[None, :] <= jnp.arange(T)[:, None]
    s_new = jnp.where(causal, s_new, NEG)
    m0 = jnp.max(s_new, axis=-1)
    p0 = jnp.exp(s_new - m0[..., None])
    l0 = jnp.sum(p0, axis=-1)
    acc0 = jnp.einsum('bhqk,bkhd->bhqd', p0, v.astype(F32))

    def step(carry, xs):
        m, l, acc = carry
        phys, suf = xs
        kp = cache_k[phys]
        vp = cache_v[phys]
        s = jnp.einsum('bqhd,bkhd->bhqk', q, kp, preferred_element_type=F32) * scale
        s = s + F_new[..., None] + suf.astype(F32).transpose(0, 2, 1)[:, :, None, :]
        m_new = jnp.maximum(m, jnp.max(s, axis=-1))
        alpha = jnp.exp(m - m_new)
        p = jnp.exp(s - m_new[..., None])
        l = l * alpha + jnp.sum(p, axis=-1)
        acc = acc * alpha[..., None] + jnp.einsum('bhqk,bkhd->bhqd', p, vp.astype(F32))
        return (m_new, l, acc), None

    xs = (page_table.T, suffix.reshape(DB, n_pages, ps, H).transpose(1, 0, 2, 3))
    (m, l, acc), _ = lax.scan(step, (m0, l0, acc0), xs)
    out = acc / l[..., None]
    return out.transpose(0, 2, 1, 3).astype(q.dtype)


def retention_chunk(q, k, v, state, log_gamma):
    C = q.shape[1]
    qf, kf, vf = q.astype(F32), k.astype(F32), v.astype(F32)
    st = state.astype(F32)
    i = jnp.arange(C, dtype=F32)
    diff = i[:, None] - i[None, :]
    decay = jnp.where(diff >= 0, jnp.exp(log_gamma[:, None, None] * jnp.maximum(diff, 0.0)), 0.0)
    inner = jnp.einsum('bihd,bjhd->bhij', qf, kf) * decay[None]
    o_inner = jnp.einsum('bhij,bjhe->bihe', inner, vf)
    q_decay = jnp.exp(log_gamma[:, None] * (i[None, :] + 1.0))
    o_cross = jnp.einsum('bihd,bhde->bihe', qf, st) * q_decay.T[None, :, :, None]
    k_decay = jnp.exp(log_gamma[:, None] * (C - 1.0 - i[None, :]))
    new_state = jnp.exp(log_gamma * C)[None, :, None, None] * st + jnp.einsum(
        'bjhd,bjhe->bhde', kf * k_decay.T[None, :, :, None], vf)
    return o_inner + o_cross, new_state


def retention_prompt(q, k, v, log_gamma):
    B, S, H, Dk = q.shape
    Dv = v.shape[-1]
    nc = S // RET_CHUNK
    to_chunks = lambda t: t.reshape(B, nc, RET_CHUNK, H, t.shape[-1]).transpose(1, 0, 2, 3, 4)

    def step(state, xs):
        qc, kc, vc = xs
        o, state = retention_chunk(qc, kc, vc, state, log_gamma)
        return state, o

    s0 = jnp.zeros((B, H, Dk, Dv), F32)
    s_fin, o = lax.scan(step, s0, (to_chunks(q), to_chunks(k), to_chunks(v)))
    return o.transpose(1, 0, 2, 3, 4).reshape(B, S, H, Dv), s_fin


def mixer_output(o_fox, o_ret, rg, ga, gb, ret_gn_w, w_o):
    B, S = o_fox.shape[0], o_fox.shape[1]
    a = o_fox.reshape(B, S, W_FOX).astype(F32)
    mu = jnp.mean(o_ret, axis=-1, keepdims=True)
    var = jnp.mean(jnp.square(o_ret - mu), axis=-1, keepdims=True)
    r = ((o_ret - mu) * lax.rsqrt(var + GN_EPS)).reshape(B, S, W_RET_V)
    r = r * ret_gn_w.astype(F32) * jax.nn.silu(rg.astype(F32))
    m = jax.nn.sigmoid(ga.astype(F32)) * a + jax.nn.sigmoid(gb.astype(F32)) * r
    return m.astype(w_o.dtype) @ w_o


def swiglu(h, w_gate, w_up, w_down):
    return (jax.nn.silu(h @ w_gate) * (h @ w_up)) @ w_down


def hybrid_layer(x, c, pos, mix_fn, w_ada, b_ada, attn_norm_w, w_in, b_f, ret_gn_w, w_o,
                 ffn_norm_w, w_gate, w_up, w_down):
    sh1, sc1, g1, sh2, sc2, g2 = ada_terms(c, w_ada, b_ada)
    h = modulate(rms_norm(x, attn_norm_w), sh1, sc1)
    fq, fk, fv, logf, rq, rk, rv, rg, ga, gb = mixer_inputs(h, w_in, b_f, pos)
    o_fox, o_ret, new_state = mix_fn(fq, fk, fv, logf, rq, rk, rv)
    x = x + (g1[:, None, :] * mixer_output(o_fox, o_ret, rg, ga, gb, ret_gn_w, w_o)).astype(x.dtype)
    h2 = modulate(rms_norm(x, ffn_norm_w), sh2, sc2)
    x = x + (g2[:, None, :] * swiglu(h2, w_gate, w_up, w_down)).astype(x.dtype)
    return x, new_state


def setup_inputs(seed: int = 0) -> dict:
    key = jax.random.key(seed)
    ks = jax.random.split(key, 24)
    n_pages = PAST_LEN // PAGE_SIZE
    n_phys = (DEC_BATCH * n_pages * 5 + 3) // 4
    nrm = lambda k, shape, s=1.0: jax.random.normal(k, shape, F32) * s
    x_prompt = nrm(ks[0], (BATCH, SEQ, D_MODEL))
    x_sample = nrm(ks[1], (DEC_BATCH, DEC_SEQ, D_MODEL))
    cache_k = nrm(ks[2], (DEPTH, n_phys, PAGE_SIZE, FOX_HEADS, FOX_HEAD_DIM))
    cache_v = nrm(ks[3], (DEPTH, n_phys, PAGE_SIZE, FOX_HEADS, FOX_HEAD_DIM))
    cache_logf = jax.nn.log_sigmoid(FORGET_BIAS + nrm(ks[4], (DEPTH, n_phys, PAGE_SIZE, FOX_HEADS), 0.5))
    state_ret = nrm(ks[5], (DEPTH, DEC_BATCH, RET_HEADS, RET_DK, RET_DV), 2.0)
    page_table = jax.random.permutation(ks[6], n_phys)[: DEC_BATCH * n_pages].reshape(
        DEC_BATCH, n_pages).astype(jnp.int32)
    c_prompt = nrm(ks[7], (BATCH, D_MODEL))
    c_sample = nrm(ks[8], (DEC_BATCH, D_MODEL))
    w_ada = nrm(ks[9], (DEPTH, D_MODEL, 6 * D_MODEL), 0.5 * D_MODEL ** -0.5)
    b_ada = nrm(ks[10], (DEPTH, 6 * D_MODEL), 0.02)
    attn_norm_w = 1.0 + nrm(ks[11], (DEPTH, D_MODEL), 0.02)
    w_in = nrm(ks[12], (DEPTH, D_MODEL, N_IN), D_MODEL ** -0.5)
    f_off = 3 * W_FOX
    w_in = w_in.at[:, :, f_off:f_off + FOX_HEADS].multiply(FORGET_W_SCALE)
    b_f = FORGET_BIAS + nrm(ks[13], (DEPTH, FOX_HEADS), 0.5)
    ret_gn_w = 1.0 + nrm(ks[14], (DEPTH, W_RET_V), 0.02)
    w_o = nrm(ks[15], (DEPTH, D_MODEL, D_MODEL), D_MODEL ** -0.5)
    ffn_norm_w = 1.0 + nrm(ks[16], (DEPTH, D_MODEL), 0.02)
    w_gate = nrm(ks[17], (DEPTH, D_MODEL, D_FF), D_MODEL ** -0.5)
    w_up = nrm(ks[18], (DEPTH, D_MODEL, D_FF), D_MODEL ** -0.5)
    w_down = nrm(ks[19], (DEPTH, D_FF, D_MODEL), D_FF ** -0.5)
    final_norm_w = 1.0 + nrm(ks[20], (D_MODEL,), 0.02)
    return {'x_prompt': x_prompt, 'x_sample': x_sample, 'cache_k': cache_k, 'cache_v': cache_v,
            'cache_logf': cache_logf, 'state_ret': state_ret, 'page_table': page_table,
            'c_prompt': c_prompt, 'c_sample': c_sample, 'w_ada': w_ada, 'b_ada': b_ada,
            'attn_norm_w': attn_norm_w, 'w_in': w_in, 'b_f': b_f, 'ret_gn_w': ret_gn_w, 'w_o': w_o,
            'ffn_norm_w': ffn_norm_w, 'w_gate': w_gate, 'w_up': w_up, 'w_down': w_down,
            'final_norm_w': final_norm_w}


def reference(x_prompt, x_sample, cache_k, cache_v, cache_logf, state_ret, page_table, c_prompt,
              c_sample, w_ada, b_ada, attn_norm_w, w_in, b_f, ret_gn_w, w_o, ffn_norm_w, w_gate,
              w_up, w_down, final_norm_w):
    seq = x_prompt.shape[1]
    dec_seq = x_sample.shape[1]
    past_len = page_table.shape[1] * cache_k.shape[2]
    pos_prompt = jnp.arange(seq, dtype=jnp.int32)
    pos_sample = past_len + jnp.arange(dec_seq, dtype=jnp.int32)
    log_gamma = retention_log_gamma()
    xp, xs = x_prompt, x_sample
    new_p, new_s = [], []
    for l in range(DEPTH):
        layer_w = (w_ada[l], b_ada[l], attn_norm_w[l], w_in[l], b_f[l], ret_gn_w[l], w_o[l],
                   ffn_norm_w[l], w_gate[l], w_up[l], w_down[l])

        def prompt_mix(fq, fk, fv, logf, rq, rk, rv):
            o_fox = fox_prompt(fq, fk, fv, logf)
            o_ret, s_fin = retention_prompt(rq, rk, rv, log_gamma)
            return o_fox, o_ret, (fk, fv, logf, s_fin)

        def sample_mix(fq, fk, fv, logf, rq, rk, rv):
            o_fox = fox_sample(fq, fk, fv, logf, cache_k[l], cache_v[l], cache_logf[l], page_table)
            o_ret, s_new = retention_chunk(rq, rk, rv, state_ret[l], log_gamma)
            return o_fox, o_ret, (fk, fv, logf, s_new)

        xp, st_p = hybrid_layer(xp, c_prompt, pos_prompt, prompt_mix, *layer_w)
        xs, st_s = hybrid_layer(xs, c_sample, pos_sample, sample_mix, *layer_w)
        new_p.append(st_p)
        new_s.append(st_s)
    y_prompt = rms_norm(xp, final_norm_w)
    y_sample = rms_norm(xs, final_norm_w)
    k_prompt = jnp.stack([s[0] for s in new_p])
    v_prompt = jnp.stack([s[1] for s in new_p])
    logf_prompt = jnp.stack([s[2] for s in new_p])
    ret_prompt = jnp.stack([s[3] for s in new_p])
    k_sample = jnp.stack([s[0] for s in new_s])
    v_sample = jnp.stack([s[1] for s in new_s])
    logf_sample = jnp.stack([s[2] for s in new_s])
    ret_sample = jnp.stack([s[3] for s in new_s])
    return (y_prompt, y_sample, k_prompt, v_prompt, logf_prompt, ret_prompt, k_sample, v_sample, logf_sample, ret_sample)
```

```python
import functools

import jax
import jax.numpy as jnp
import numpy as np
from jax import lax
from jax.experimental import pallas as pl
from jax.experimental.pallas import tpu as pltpu

F32 = jnp.float32
BF16 = jnp.bfloat16

NORM_EPS = 1e-6
GN_EPS = 1e-5
ROPE_BASE = 10000.0
NEG = -1e30
RET_CHUNK = 128
LANES = 128
MIB = 1 << 20


def _params(semantics, vmem_mib):
    return pltpu.CompilerParams(dimension_semantics=semantics, vmem_limit_bytes=vmem_mib * MIB)


def _nt_dot(a, b):
    return lax.dot_general(a, b, (((1,), (1,)), ((), ())), preferred_element_type=F32)


def _tn_dot(a, b):
    return lax.dot_general(a, b, (((0,), (0,)), ((), ())), preferred_element_type=F32)


def _dot(a, b):
    return jnp.dot(a, b, preferred_element_type=F32)


def _sigmoid(x):
    return 1.0 / (1.0 + jnp.exp(-x))


def _silu(x):
    return x * _sigmoid(x)


def _rms(x, w):
    return x * lax.rsqrt(jnp.mean(x * x, axis=-1, keepdims=True) + NORM_EPS) * w


def _split_bf16(x):
    hi = x.astype(BF16)
    lo = (x - hi.astype(F32)).astype(BF16)
    return hi, lo


def _row_spec(tm, width):
    return pl.BlockSpec((tm, width), lambda i: (i, 0))


def _const_spec(shape):
    return pl.BlockSpec(shape, lambda *_: (0,) * len(shape))


def _mod_spec(arr, tm, rows_per_batch):
    if arr.ndim == 3:
        tiles = rows_per_batch // tm
        return pl.BlockSpec((None, 1, arr.shape[-1]), lambda i: (i // tiles, 0, 0))
    return _row_spec(tm, arr.shape[-1])


def _ada_kernel(c_ref, w_ref, b_ref, o_ref):
    a = _silu(c_ref[...]).astype(BF16)
    o_ref[...] = _dot(a, w_ref[...].astype(BF16)) + b_ref[...]


def _ada(c, w, b, tn=1536):
    rows, d = c.shape
    n = w.shape[1]
    return pl.pallas_call(
        _ada_kernel,
        grid=(n // tn,),
        in_specs=[_const_spec((rows, d)), pl.BlockSpec((d, tn), lambda j: (0, j)),
                  pl.BlockSpec((1, tn), lambda j: (0, j))],
        out_specs=pl.BlockSpec((rows, tn), lambda j: (0, j)),
        out_shape=jax.ShapeDtypeStruct((rows, n), F32),
        compiler_params=_params(("parallel",), 40),
        name="ada",
    )(c, w, b.reshape(1, n))


def _norm_mod_kernel(x_ref, nw_ref, sc_ref, sh_ref, o_ref):
    y = _rms(x_ref[...], nw_ref[...])
    o_ref[...] = (y * (1.0 + sc_ref[...]) + sh_ref[...]).astype(o_ref.dtype)


def _norm_mod(x, nw, sc, sh, rows_per_batch, tm):
    rows, d = x.shape
    return pl.pallas_call(
        _norm_mod_kernel,
        grid=(rows // tm,),
        in_specs=[_row_spec(tm, d), _const_spec((1, d)), _mod_spec(sc, tm, rows_per_batch),
                  _mod_spec(sh, tm, rows_per_batch)],
        out_specs=_row_spec(tm, d),
        out_shape=jax.ShapeDtypeStruct((rows, d), BF16),
        compiler_params=_params(("parallel",), 32),
        name="norm_mod",
    )(x, nw.reshape(1, d), sc, sh)


def _proj_kernel(h_ref, w_ref, *o_refs, scale):
    z = _dot(h_ref[...], w_ref[...])
    if scale != 1.0:
        z = z * scale
    for o_ref in o_refs:
        o_ref[...] = z.astype(o_ref.dtype)


def _proj(h, w, dtypes, tm, scale=1.0):
    rows, d = h.shape
    n = w.shape[1]
    outs = pl.pallas_call(
        functools.partial(_proj_kernel, scale=scale),
        grid=(rows // tm,),
        in_specs=[_row_spec(tm, d), _const_spec((d, n))],
        out_specs=[_row_spec(tm, n) for _ in dtypes],
        out_shape=[jax.ShapeDtypeStruct((rows, n), dt) for dt in dtypes],
        compiler_params=_params(("parallel",), 40),
        name="proj",
    )(h, w)
    return outs


def _proj_rope_kernel(h_ref, w_ref, cos_ref, sin_ref, o_ref, *, scale, head_dim):
    z = _dot(h_ref[...], w_ref[...])
    cos = cos_ref[...]
    sin = sin_ref[...]
    for hd in range(z.shape[1] // head_dim):
        zh = z[:, hd * head_dim:(hd + 1) * head_dim]
        rot = zh * cos + pltpu.roll(zh, head_dim // 2, 1) * sin
        o_ref[:, hd * head_dim:(hd + 1) * head_dim] = (rot * scale).astype(o_ref.dtype)


def _proj_rope(h, w, cos2, sin2, pos_tiles, tm, scale, head_dim):
    rows, d = h.shape
    n = w.shape[1]
    tab = pl.BlockSpec((tm, head_dim), lambda i: (i % pos_tiles, 0))
    return pl.pallas_call(
        functools.partial(_proj_rope_kernel, scale=scale, head_dim=head_dim),
        grid=(rows // tm,),
        in_specs=[_row_spec(tm, d), _const_spec((d, n)), tab, tab],
        out_specs=_row_spec(tm, n),
        out_shape=jax.ShapeDtypeStruct((rows, n), BF16),
        compiler_params=_params(("parallel",), 40),
        name="proj_rope",
    )(h, w, cos2, sin2)


def _log_sigmoid(x):
    return jnp.minimum(x, 0.0) - jnp.log1p(jnp.exp(-jnp.abs(x)))


def _proj_logf_kernel(h_ref, w_ref, b_ref, o_ref):
    z = _dot(h_ref[...], w_ref[...]) + b_ref[...]
    o_ref[...] = _log_sigmoid(z)[:, :o_ref.shape[1]]


def _proj_logf(h, w_pad, b_pad, heads, tm):
    rows, d = h.shape
    n = w_pad.shape[1]
    return pl.pallas_call(
        _proj_logf_kernel,
        grid=(rows // tm,),
        in_specs=[_row_spec(tm, d), _const_spec((d, n)), _const_spec((1, n))],
        out_specs=_row_spec(tm, heads),
        out_shape=jax.ShapeDtypeStruct((rows, heads), F32),
        compiler_params=_params(("parallel",), 32),
        name="proj_logf",
    )(h, w_pad, b_pad)


def _cumsum_kernel(lf_ref, tri_ref, o_ref, carry_ref):
    @pl.when(pl.program_id(1) == 0)
    def _():
        carry_ref[...] = jnp.zeros_like(carry_ref)

    hi, lo = _split_bf16(lf_ref[...])
    tri = tri_ref[...]
    f = _dot(tri, hi) + _dot(tri, lo) + carry_ref[...]
    o_ref[...] = f
    carry_ref[...] = f[f.shape[0] - 1:, :]


def _cumsum_seq(lf, batch, tc=512):
    rows, heads = lf.shape
    nt = rows // batch // tc
    tri = (jnp.arange(tc)[:, None] >= jnp.arange(tc)[None, :]).astype(BF16)
    return pl.pallas_call(
        _cumsum_kernel,
        grid=(batch, nt),
        in_specs=[pl.BlockSpec((tc, heads), lambda b, t: (b * nt + t, 0)), _const_spec((tc, tc))],
        out_specs=pl.BlockSpec((tc, heads), lambda b, t: (b * nt + t, 0)),
        out_shape=jax.ShapeDtypeStruct((rows, heads), F32),
        scratch_shapes=[pltpu.VMEM((1, heads), F32)],
        compiler_params=_params(("parallel", "arbitrary"), 32),
        name="cumsum_logf",
    )(lf, tri)


def _fox_prompt_kernel(q_ref, k_ref, v_ref, fk_ref, o_ref, m_ref, l_ref, acc_ref, *, tq, head_dim):
    qi = pl.program_id(2)
    q = q_ref[...]
    lane = lax.broadcasted_iota(jnp.int32, q.shape, 1)
    zero = jnp.zeros_like(q)
    q_heads = (jnp.where(lane < head_dim, q, zero), jnp.where(lane >= head_dim, q, zero))
    f_ref0 = fk_ref[qi][:, 0:1]

    m_ref[...] = jnp.full_like(m_ref, NEG)
    l_ref[...] = jnp.zeros_like(l_ref)
    acc_ref[...] = jnp.zeros_like(acc_ref)

    def update(j, masked):
        start = pl.multiple_of(j * tq, tq)
        ks = k_ref[pl.ds(start, tq), :]
        vs = v_ref[pl.ds(start, tq), :]
        bias = f_ref0 - fk_ref[j]
        if masked:
            row = lax.broadcasted_iota(jnp.int32, (tq, tq), 0)
            col = lax.broadcasted_iota(jnp.int32, (tq, tq), 1)
            keep = col <= row
        for hd in range(2):
            s = _nt_dot(q_heads[hd], ks) + bias[hd:hd + 1, :]
            if masked:
                s = jnp.where(keep, s, NEG)
            m_prev = m_ref[hd]
            m_new = jnp.maximum(m_prev, jnp.max(s, axis=1, keepdims=True))
            alpha = jnp.exp(m_prev - m_new)
            p = jnp.exp(s - m_new)
            l_ref[hd] = alpha * l_ref[hd] + jnp.sum(p, axis=1, keepdims=True)
            acc_ref[hd] = alpha * acc_ref[hd] + _dot(p.astype(BF16), vs)
            m_ref[hd] = m_new

    def body(j, carry):
        update(j, False)
        return carry

    lax.fori_loop(0, qi, body, 0)
    update(qi, True)

    out = jnp.where(lane < head_dim, acc_ref[0] / l_ref[0], acc_ref[1] / l_ref[1])
    o_ref[...] = out.astype(o_ref.dtype)


def _fox_prompt(q, k, v, f_cum, batch, seq, heads, head_dim, tq=256):
    assert 2 * head_dim == LANES
    pairs = heads // 2
    nq = seq // tq
    width = heads * head_dim
    q3, k3, v3 = (t.reshape(batch, seq, width) for t in (q, k, v))
    fk = f_cum.reshape(batch, nq, tq, pairs, 2).transpose(0, 3, 1, 4, 2)
    out = pl.pallas_call(
        functools.partial(_fox_prompt_kernel, tq=tq, head_dim=head_dim),
        grid=(batch, pairs, nq),
        in_specs=[
            pl.BlockSpec((None, tq, LANES), lambda b, p, i: (b, i, p)),
            pl.BlockSpec((None, seq, LANES), lambda b, p, i: (b, 0, p)),
            pl.BlockSpec((None, seq, LANES), lambda b, p, i: (b, 0, p)),
            pl.BlockSpec((None, None, nq, 2, tq), lambda b, p, i: (b, p, 0, 0, 0)),
        ],
        out_specs=pl.BlockSpec((None, tq, LANES), lambda b, p, i: (b, i, p)),
        out_shape=jax.ShapeDtypeStruct((batch, seq, width), BF16),
        scratch_shapes=[pltpu.VMEM((2, tq, 1), F32), pltpu.VMEM((2, tq, 1), F32),
                        pltpu.VMEM((2, tq, LANES), F32)],
        compiler_params=_params(("parallel", "parallel", "arbitrary"), 32),
        name="fox_prompt",
    )(q3, k3, v3, fk)
    return out.reshape(batch * seq, width)


def _retention_kernel(q_ref, k_ref, v_ref, g_ref, s0_ref, dec_ref, qd_ref, kd_ref, gc_ref, gnw_ref,
                      r_ref, sfin_ref, st_ref, *, heads, dk, dv):
    c = pl.program_id(1)

    @pl.when(c == 0)
    def _():
        st_ref[...] = s0_ref[...]

    for hd in range(heads):
        ksl = slice(hd * dk, (hd + 1) * dk)
        vsl = slice(hd * dv, (hd + 1) * dv)
        q = q_ref[:, ksl]
        k = k_ref[:, ksl]
        v = v_ref[:, vsl]
        st = st_ref[hd]
        inner = _nt_dot(q, k) * dec_ref[hd]
        o = _dot(inner.astype(BF16), v) + _dot(q, st.astype(BF16)) * qd_ref[hd]
        k_dec = (k.astype(F32) * kd_ref[hd]).astype(BF16)
        st_ref[hd] = gc_ref[hd] * st + _tn_dot(k_dec, v)
        mu = jnp.mean(o, axis=-1, keepdims=True)
        cen = o - mu
        var = jnp.mean(cen * cen, axis=-1, keepdims=True)
        r = cen * lax.rsqrt(var + GN_EPS) * gnw_ref[:, vsl] * _silu(g_ref[:, vsl].astype(F32))
        r_ref[:, vsl] = r.astype(r_ref.dtype)

    @pl.when(c == pl.num_programs(1) - 1)
    def _():
        sfin_ref[...] = st_ref[...]


def _retention(q, k, v, g, state0, tables, gn_w, chunk):
    batch, seq, _ = q.shape
    _, heads, dk, dv = state0.shape
    dec, qd, kd, gc = tables
    nc = seq // chunk
    tok = lambda w: pl.BlockSpec((None, chunk, w), lambda b, c: (b, c, 0))
    st_spec = pl.BlockSpec((None, heads, dk, dv), lambda b, c: (b, 0, 0, 0))
    return pl.pallas_call(
        functools.partial(_retention_kernel, heads=heads, dk=dk, dv=dv),
        grid=(batch, nc),
        in_specs=[tok(heads * dk), tok(heads * dk), tok(heads * dv), tok(heads * dv), st_spec,
                  _const_spec(dec.shape), _const_spec(qd.shape), _const_spec(kd.shape),
                  _const_spec(gc.shape), _const_spec((1, heads * dv))],
        out_specs=[tok(heads * dv), st_spec],
        out_shape=[jax.ShapeDtypeStruct((batch, seq, heads * dv), BF16),
                   jax.ShapeDtypeStruct(state0.shape, F32)],
        scratch_shapes=[pltpu.VMEM((heads, dk, dv), F32)],
        compiler_params=_params(("parallel", "arbitrary"), 32),
        name="retention",
    )(q, k, v, g, state0, dec, qd, kd, gc, gn_w.reshape(1, heads * dv))


def _retention_tables(heads, chunk, padded):
    log_gamma = jnp.log(1.0 - 2.0 ** (-5.0 - jnp.arange(heads, dtype=F32)))
    i = jnp.arange(chunk, dtype=F32)
    diff = i[:, None] - i[None, :]
    dec = jnp.where(diff >= 0, jnp.exp(log_gamma[:, None, None] * jnp.maximum(diff, 0.0)), 0.0)
    qd = jnp.exp(log_gamma[:, None] * (i[None, :] + 1.0))[:, :, None]
    kd = jnp.exp(log_gamma[:, None] * (chunk - 1.0 - i[None, :]))[:, :, None]
    gc = jnp.exp(log_gamma * chunk)[:, None, None]
    pad = padded - chunk
    dec = jnp.pad(dec, ((0, 0), (0, pad), (0, pad)))
    qd = jnp.pad(qd, ((0, 0), (0, pad), (0, 0)))
    kd = jnp.pad(kd, ((0, 0), (0, pad), (0, 0)))
    return dec, qd, kd, gc


def _fox_sample_kernel(pt_ref, qbd_ref, kn_ref, vn_ref, lfn_ref, usuf_ref, uinc_ref, *rest,
                       pages_per_step, heads, head_dim, new_tokens, page):
    g_n = pages_per_step
    k_refs = rest[:g_n]
    v_refs = rest[g_n:2 * g_n]
    lf_refs = rest[2 * g_n:3 * g_n]
    o_ref, m_ref, l_ref, acc_ref, carry_ref, bias_ref = rest[3 * g_n:]
    j = pl.program_id(1)
    rows = new_tokens * heads
    qbd = qbd_ref[...]

    def bias_rows(lf, u):
        hi, lo = _split_bf16(lf)
        return _tn_dot(hi, u) + _tn_dot(lo, u)

    def set_bias(per_head):
        for t in range(new_tokens):
            bias_ref[t * heads:(t + 1) * heads, :] = per_head

    def update(kp, vp, keep):
        s = _nt_dot(qbd, kp) + bias_ref[...]
        if keep is not None:
            s = jnp.where(keep, s, NEG)
        m_prev = m_ref[...]
        m_new = jnp.maximum(m_prev, jnp.max(s, axis=1, keepdims=True))
        alpha = jnp.exp(m_prev - m_new)
        p = jnp.exp(s - m_new)
        l_ref[...] = alpha * l_ref[...] + jnp.sum(p, axis=1, keepdims=True)
        acc_ref[...] = alpha * acc_ref[...] + _dot(p.astype(BF16), vp)
        m_ref[...] = m_new

    @pl.when(j == 0)
    def _():
        m_ref[...] = jnp.full_like(m_ref, NEG)
        l_ref[...] = jnp.zeros_like(l_ref)
        acc_ref[...] = jnp.zeros_like(acc_ref)
        carry_ref[...] = jnp.zeros_like(carry_ref)
        kn = kn_ref[...]
        zpad = jnp.zeros((page - kn.shape[0], kn.shape[1]), kn.dtype)
        kp = jnp.concatenate([kn, zpad], axis=0)
        vp = jnp.concatenate([vn_ref[...], zpad], axis=0)
        lfn = lfn_ref[...]
        lf = jnp.concatenate([lfn, jnp.zeros((page - lfn.shape[0], lfn.shape[1]), F32)], axis=0)
        set_bias(bias_rows(lf, uinc_ref[...])[:, :page])
        row = lax.broadcasted_iota(jnp.int32, (rows, page), 0)
        col = lax.broadcasted_iota(jnp.int32, (rows, page), 1)
        update(kp, vp, col * heads <= row)

    for g in range(g_n):
        b16 = bias_rows(lf_refs[g][...], usuf_ref[...])
        carry = carry_ref[...]
        set_bias(b16[:, :page] + carry)
        update(k_refs[g][...].astype(BF16), v_refs[g][...].astype(BF16), None)
        carry_ref[...] = carry + b16[:, page:page + 1]

    @pl.when(j == pl.num_programs(1) - 1)
    def _():
        out = acc_ref[...] / l_ref[...]
        row = lax.broadcasted_iota(jnp.int32, out.shape, 0)
        col = lax.broadcasted_iota(jnp.int32, out.shape, 1)
        own = (row % heads) == (col // head_dim)
        out = jnp.where(own, out, 0.0)
        o_ref[...] = out.reshape(new_tokens, heads, out.shape[1]).sum(axis=1)


def _fox_sample(q, k_new, v_new, lf_new, cache_k, cache_v, cache_lf, page_table, pages_per_step=4):
    batch, new_tokens, width = q.shape
    heads = lf_new.shape[-1]
    head_dim = width // heads
    n_pages = page_table.shape[1]
    page = cache_k.shape[1]
    rows = new_tokens * heads
    g_n = pages_per_step
    assert n_pages % g_n == 0 and page == LANES

    eye = jnp.eye(heads, dtype=q.dtype)
    qbd = (q.reshape(batch, new_tokens, heads, 1, head_dim) * eye[None, None, :, :, None]
           ).reshape(batch, rows, width)
    pad_t = 16 - new_tokens
    k_new = jnp.pad(k_new, ((0, 0), (0, pad_t), (0, 0)))
    v_new = jnp.pad(v_new, ((0, 0), (0, pad_t), (0, 0)))
    lf_new = jnp.pad(lf_new, ((0, 0), (0, 8 - new_tokens), (0, 0)))

    idx = jnp.arange(page)
    ones_col = (jnp.arange(page)[None, :] == 0).astype(F32) * jnp.ones((page, 1), F32)
    usuf = jnp.concatenate([(idx[:, None] > idx[None, :]).astype(F32), ones_col], axis=1).astype(BF16)
    uinc = jnp.concatenate([-(idx[:, None] <= idx[None, :]).astype(F32), ones_col], axis=1).astype(BF16)

    def page_map(g):
        return lambda b, j, pt: (pt[b, n_pages - 1 - (j * g_n + g)], 0, 0)

    per_b = lambda shape: pl.BlockSpec((None,) + shape, lambda b, j, pt: (b, 0, 0))
    in_specs = [per_b((rows, width)), per_b((16, width)), per_b((16, width)), per_b((8, heads)),
                pl.BlockSpec(usuf.shape, lambda b, j, pt: (0, 0)),
                pl.BlockSpec(uinc.shape, lambda b, j, pt: (0, 0))]
    in_specs += [pl.BlockSpec((None, page, width), page_map(g)) for g in range(g_n)]
    in_specs += [pl.BlockSpec((None, page, width), page_map(g)) for g in range(g_n)]
    in_specs += [pl.BlockSpec((None, page, heads), page_map(g)) for g in range(g_n)]
    grid_spec = pltpu.PrefetchScalarGridSpec(
        num_scalar_prefetch=1,
        grid=(batch, n_pages // g_n),
        in_specs=in_specs,
        out_specs=pl.BlockSpec((None, new_tokens, width), lambda b, j, pt: (b, 0, 0)),
        scratch_shapes=[pltpu.VMEM((rows, 1), F32), pltpu.VMEM((rows, 1), F32),
                        pltpu.VMEM((rows, width), F32), pltpu.VMEM((heads, 1), F32),
                        pltpu.VMEM((rows, page), F32)],
    )
    return pl.pallas_call(
        functools.partial(_fox_sample_kernel, pages_per_step=g_n, heads=heads, head_dim=head_dim,
                          new_tokens=new_tokens, page=page),
        grid_spec=grid_spec,
        out_shape=jax.ShapeDtypeStruct((batch, new_tokens, width), F32),
        compiler_params=_params(("parallel", "arbitrary"), 48),
        name="fox_sample",
    )(page_table, qbd, k_new, v_new, lf_new, usuf, uinc,
      *([cache_k] * g_n), *([cache_v] * g_n), *([cache_lf] * g_n))


def _mix_out_kernel(a_ref, r_ref, ga_ref, gb_ref, x_ref, g1_ref, wo_ref, o_ref):
    m = (_sigmoid(ga_ref[...].astype(F32)) * a_ref[...].astype(F32)
         + _sigmoid(gb_ref[...].astype(F32)) * r_ref[...].astype(F32))
    o_ref[...] = x_ref[...] + g1_ref[...] * _dot(m.astype(BF16), wo_ref[...])


def _mix_out(a, r, ga, gb, x, g1, wo, rows_per_batch, tm):
    rows, d = x.shape
    w = a.shape[1]
    return pl.pallas_call(
        _mix_out_kernel,
        grid=(rows // tm,),
        in_specs=[_row_spec(tm, w), _row_spec(tm, w), _row_spec(tm, w), _row_spec(tm, w),
                  _row_spec(tm, d), _mod_spec(g1, tm, rows_per_batch), _const_spec(wo.shape)],
        out_specs=_row_spec(tm, d),
        out_shape=jax.ShapeDtypeStruct((rows, d), F32),
        compiler_params=_params(("parallel",), 48),
        name="mix_out",
    )(a, r, ga, gb, x, g1, wo)


def _ffn_kernel(x_ref, nw_ref, sc_ref, sh_ref, g2_ref, wg_ref, wu_ref, wd_ref, fw_ref, o_ref, acc_ref,
                *, chunk, final_norm):
    x = x_ref[...]
    h = (_rms(x, nw_ref[...]) * (1.0 + sc_ref[...]) + sh_ref[...]).astype(BF16)
    d_ff = wg_ref.shape[1]
    for c in range(d_ff // chunk):
        sl = slice(c * chunk, (c + 1) * chunk)
        gate = _dot(h, wg_ref[:, sl])
        up = _dot(h, wu_ref[:, sl])
        part = _dot((_silu(gate) * up).astype(BF16), wd_ref[sl, :])
        if c == 0:
            acc_ref[...] = part
        else:
            acc_ref[...] += part
    x2 = x + g2_ref[...] * acc_ref[...]
    o_ref[...] = _rms(x2, fw_ref[...]) if final_norm else x2


def _ffn(x, nw, sc, sh, g2, wg, wu, wd, fw, final_norm, rows_per_batch, tm, chunk=256):
    rows, d = x.shape
    d_ff = wg.shape[1]
    assert d_ff % chunk == 0
    once = lambda shape: pl.BlockSpec(shape, lambda *_: (0,) * len(shape), pipeline_mode=pl.Buffered(1))
    return pl.pallas_call(
        functools.partial(_ffn_kernel, chunk=chunk, final_norm=final_norm),
        grid=(rows // tm,),
        in_specs=[_row_spec(tm, d), _const_spec((1, d)), _mod_spec(sc, tm, rows_per_batch),
                  _mod_spec(sh, tm, rows_per_batch), _mod_spec(g2, tm, rows_per_batch),
                  once((d, d_ff)), once((d, d_ff)), once((d_ff, d)), _const_spec((1, d))],
        out_specs=_row_spec(tm, d),
        out_shape=jax.ShapeDtypeStruct((rows, d), F32),
        scratch_shapes=[pltpu.VMEM((tm, d), F32)],
        compiler_params=_params(("parallel",), 52),
        name="ffn",
    )(x, nw.reshape(1, d), sc, sh, g2, wg, wu, wd, fw.reshape(1, d))


def _rope_tables(pos, head_dim):
    half = head_dim // 2
    inv = ROPE_BASE ** (-jnp.arange(half, dtype=F32) / half)
    ang = pos.astype(F32)[:, None] * inv[None, :]
    cos, sin = jnp.cos(ang), jnp.sin(ang)
    return jnp.concatenate([cos, cos], axis=-1), jnp.concatenate([-sin, sin], axis=-1)


def _group_layer(x, rows_per_batch, mods, rope_tabs, pos_tiles, tm, tm_proj, lw, dims, mixer):
    sh1, sc1, g1, sh2, sc2, g2 = mods
    fox_heads, fox_dim, ret_heads, ret_dk, ret_dv = dims
    cos2, sin2 = rope_tabs
    h = _norm_mod(x, lw["attn_norm_w"], sc1, sh1, rows_per_batch, tm)
    (fq,) = _proj(h, lw["w_fq"], (BF16,), tm_proj, scale=fox_dim ** -0.5)
    fk32, fk16 = _proj(h, lw["w_fk"], (F32, BF16), tm_proj)
    fv32, fv16 = _proj(h, lw["w_fv"], (F32, BF16), tm_proj)
    logf = _proj_logf(h, lw["w_f"], lw["b_f"], fox_heads, tm_proj)
    rq = _proj_rope(h, lw["w_rq"], cos2, sin2, pos_tiles, tm_proj, ret_dk ** -0.5, ret_dk)
    rk = _proj_rope(h, lw["w_rk"], cos2, sin2, pos_tiles, tm_proj, 1.0, ret_dk)
    (rv,) = _proj(h, lw["w_rv"], (BF16,), tm_proj)
    (rg,) = _proj(h, lw["w_rg"], (F32,), tm_proj)
    (ga,) = _proj(h, lw["w_ga"], (F32,), tm_proj)
    (gb,) = _proj(h, lw["w_gb"], (F32,), tm_proj)
    a, r, s_fin = mixer(fq, fk16, fv16, logf, rq, rk, rv, rg)
    x1 = _mix_out(a, r, ga, gb, x, g1, lw["w_o"], rows_per_batch, tm)
    return x1, (fk32, fv32, logf, s_fin)


def kernel(x_prompt, x_sample, cache_k, cache_v, cache_logf, state_ret, page_table, c_prompt, c_sample,
           w_ada, b_ada, attn_norm_w, w_in, b_f, ret_gn_w, w_o, ffn_norm_w, w_gate, w_up, w_down,
           final_norm_w):
    batch, seq, d = x_prompt.shape
    dec_batch, dec_seq, _ = x_sample.shape
    depth, n_phys, page, fox_heads, fox_dim = cache_k.shape
    _, _, ret_heads, ret_dk, ret_dv = state_ret.shape
    n_pages = page_table.shape[1]
    past_len = n_pages * page
    w_fox = fox_heads * fox_dim
    w_rk = ret_heads * ret_dk
    w_rv = ret_heads * ret_dv
    dims = (fox_heads, fox_dim, ret_heads, ret_dk, ret_dv)
    splits = (w_fox, w_fox, w_fox, fox_heads, w_rk, w_rk, w_rv, w_rv, d, d)
    offs = np.concatenate([[0], np.cumsum(splits)])
    seg_names = ("w_fq", "w_fk", "w_fv", "w_f", "w_rq", "w_rk", "w_rv", "w_rg", "w_ga", "w_gb")

    xp = x_prompt.reshape(batch * seq, d)
    xs = x_sample.reshape(dec_batch * dec_seq, d)
    rows_s = dec_batch * dec_seq

    tm_p, tm_proj_p = 512, 1024
    rope_p = _rope_tables(jnp.arange(seq, dtype=jnp.int32), ret_dk)
    pos_s = past_len + jnp.arange(dec_seq, dtype=jnp.int32)
    rope_s = tuple(jnp.tile(t, (dec_batch, 1)) for t in _rope_tables(pos_s, ret_dk))
    tab_p = _retention_tables(ret_heads, RET_CHUNK, RET_CHUNK)
    dec_pad = 16
    tab_s = _retention_tables(ret_heads, dec_seq, dec_pad)

    ck = cache_k.reshape(depth * n_phys, page, w_fox)
    cv = cache_v.reshape(depth * n_phys, page, w_fox)
    clf = cache_logf.reshape(depth * n_phys, page, fox_heads)

    c_all = jnp.concatenate([c_prompt, c_sample], axis=0)
    c_rows = -(-c_all.shape[0] // 16) * 16
    c_all = jnp.pad(c_all, ((0, c_rows - c_all.shape[0]), (0, 0)))

    new_p, new_s = [], []
    for l in range(depth):
        mod = _ada(c_all, w_ada[l], b_ada[l])
        mods_p = tuple(m.reshape(batch, 1, d) for m in jnp.split(mod[:batch], 6, axis=-1))
        mods_s = tuple(jnp.repeat(m, dec_seq, axis=0)
                       for m in jnp.split(mod[batch:batch + dec_batch], 6, axis=-1))
        lw = {n: w_in[l][:, offs[i]:offs[i + 1]].astype(BF16) for i, n in enumerate(seg_names)}
        lw["w_f"] = jnp.pad(lw["w_f"], ((0, 0), (0, LANES - fox_heads)))
        lw["b_f"] = jnp.pad(b_f[l].astype(F32), (0, LANES - fox_heads)).reshape(1, LANES)
        lw["attn_norm_w"] = attn_norm_w[l]
        lw["w_o"] = w_o[l].astype(BF16)
        wg, wu, wd = w_gate[l].astype(BF16), w_up[l].astype(BF16), w_down[l].astype(BF16)

        def prompt_mixer(fq, fk, fv, logf, rq, rk, rv, rg):
            f_cum = _cumsum_seq(logf, batch)
            a = _fox_prompt(fq, fk, fv, f_cum, batch, seq, fox_heads, fox_dim)
            to3 = lambda t: t.reshape(batch, seq, t.shape[-1])
            s0 = jnp.zeros((batch, ret_heads, ret_dk, ret_dv), F32)
            r, s_fin = _retention(to3(rq), to3(rk), to3(rv), to3(rg), s0, tab_p, ret_gn_w[l], RET_CHUNK)
            return a, r.reshape(batch * seq, w_rv), s_fin

        def sample_mixer(fq, fk, fv, logf, rq, rk, rv, rg):
            to3 = lambda t: t.reshape(dec_batch, dec_seq, t.shape[-1])
            page_ids = page_table + l * n_phys
            a = _fox_sample(to3(fq), to3(fk), to3(fv), to3(logf), ck, cv, clf, page_ids)
            padt = lambda t: jnp.pad(to3(t), ((0, 0), (0, dec_pad - dec_seq), (0, 0)))
            r, s_new = _retention(padt(rq), padt(rk), padt(rv), padt(rg), state_ret[l], tab_s,
                                  ret_gn_w[l], dec_pad)
            return a.reshape(rows_s, w_fox), r[:, :dec_seq].reshape(rows_s, w_rv), s_new

        xp1, st_p = _group_layer(xp, seq, mods_p, rope_p, seq // tm_proj_p, tm_p, tm_proj_p, lw, dims,
                                 prompt_mixer)
        xs1, st_s = _group_layer(xs, dec_seq, mods_s, rope_s, 1, rows_s, rows_s, lw, dims, sample_mixer)
        last = l == depth - 1
        xp = _ffn(xp1, ffn_norm_w[l], mods_p[4], mods_p[3], mods_p[5], wg, wu, wd, final_norm_w, last,
                  seq, tm_p)
        xs = _ffn(xs1, ffn_norm_w[l], mods_s[4], mods_s[3], mods_s[5], wg, wu, wd, final_norm_w, last,
                  dec_seq, rows_s)
        new_p.append(st_p)
        new_s.append(st_s)

    y_prompt = xp.reshape(batch, seq, d)
    y_sample = xs.reshape(dec_batch, dec_seq, d)
    kv_p = lambda t: t.reshape(batch, seq, fox_heads, fox_dim)
    kv_s = lambda t: t.reshape(dec_batch, dec_seq, fox_heads, fox_dim)
    k_prompt = jnp.stack([kv_p(s[0]) for s in new_p])
    v_prompt = jnp.stack([kv_p(s[1]) for s in new_p])
    logf_prompt = jnp.stack([s[2].reshape(batch, seq, fox_heads) for s in new_p])
    ret_prompt = jnp.stack([s[3] for s in new_p])
    k_sample = jnp.stack([kv_s(s[0]) for s in new_s])
    v_sample = jnp.stack([kv_s(s[1]) for s in new_s])
    logf_sample = jnp.stack([s[2].reshape(dec_batch, dec_seq, fox_heads) for s in new_s])
    ret_sample = jnp.stack([s[3] for s in new_s])
    return (y_prompt, y_sample, k_prompt, v_prompt, logf_prompt, ret_prompt, k_sample, v_sample,
            logf_sample, ret_sample)
```

```python
import functools

import jax
import jax.numpy as jnp
import numpy as np
from jax import lax
from jax.experimental import pallas as pl
from jax.experimental.pallas import tpu as pltpu

F32 = jnp.float32
BF16 = jnp.bfloat16

NORM_EPS = 1e-6
GN_EPS = 1e-5
ROPE_BASE = 10000.0
NEG = -1e30
RET_CHUNK = 128
LANES = 128
MIB = 1 << 20


def _params(semantics, vmem_mib):
    return pltpu.CompilerParams(dimension_semantics=semantics, vmem_limit_bytes=vmem_mib * MIB)


def _nt_dot(a, b):
    return lax.dot_general(a, b, (((1,), (1,)), ((), ())), preferred_element_type=F32)


def _tn_dot(a, b):
    return lax.dot_general(a, b, (((0,), (0,)), ((), ())), preferred_element_type=F32)


def _dot(a, b):
    return jnp.dot(a, b, preferred_element_type=F32)


def _sigmoid(x):
    return 1.0 / (1.0 + jnp.exp(-x))


def _silu(x):
    return x * _sigmoid(x)


def _rms(x, w):
    return x * lax.rsqrt(jnp.mean(x * x, axis=-1, keepdims=True) + NORM_EPS) * w


def _split_bf16(x):
    hi = x.astype(BF16)
    lo = (x - hi.astype(F32)).astype(BF16)
    return hi, lo


def _row_spec(tm, width):
    return pl.BlockSpec((tm, width), lambda i: (i, 0))


def _const_spec(shape):
    return pl.BlockSpec(shape, lambda *_: (0,) * len(shape))


def _mod_spec(arr, tm, rows_per_batch):
    if arr.ndim == 3:
        tiles = rows_per_batch // tm
        return pl.BlockSpec((None, 1, arr.shape[-1]), lambda i: (i // tiles, 0, 0))
    return _row_spec(tm, arr.shape[-1])


def _ada_kernel(c_ref, w_ref, b_ref, o_ref):
    a = _silu(c_ref[...]).astype(BF16)
    o_ref[...] = _dot(a, w_ref[...].astype(BF16)) + b_ref[...]


def _ada(c, w, b, tn=1536):
    rows, d = c.shape
    n = w.shape[1]
    return pl.pallas_call(
        _ada_kernel,
        grid=(n // tn,),
        in_specs=[_const_spec((rows, d)), pl.BlockSpec((d, tn), lambda j: (0, j)),
                  pl.BlockSpec((1, tn), lambda j: (0, j))],
        out_specs=pl.BlockSpec((rows, tn), lambda j: (0, j)),
        out_shape=jax.ShapeDtypeStruct((rows, n), F32),
        compiler_params=_params(("parallel",), 40),
        name="ada",
    )(c, w, b.reshape(1, n))


def _norm_mod_kernel(x_ref, nw_ref, sc_ref, sh_ref, o_ref):
    y = _rms(x_ref[...], nw_ref[...])
    o_ref[...] = (y * (1.0 + sc_ref[...]) + sh_ref[...]).astype(o_ref.dtype)


def _norm_mod(x, nw, sc, sh, rows_per_batch, tm):
    rows, d = x.shape
    return pl.pallas_call(
        _norm_mod_kernel,
        grid=(rows // tm,),
        in_specs=[_row_spec(tm, d), _const_spec((1, d)), _mod_spec(sc, tm, rows_per_batch),
                  _mod_spec(sh, tm, rows_per_batch)],
        out_specs=_row_spec(tm, d),
        out_shape=jax.ShapeDtypeStruct((rows, d), BF16),
        compiler_params=_params(("parallel",), 32),
        name="norm_mod",
    )(x, nw.reshape(1, d), sc, sh)


def _proj_kernel(h_ref, w_ref, *o_refs, scale):
    z = _dot(h_ref[...], w_ref[...])
    if scale != 1.0:
        z = z * scale
    for o_ref in o_refs:
        o_ref[...] = z.astype(o_ref.dtype)


def _proj(h, w, dtypes, tm, scale=1.0):
    rows, d = h.shape
    n = w.shape[1]
    outs = pl.pallas_call(
        functools.partial(_proj_kernel, scale=scale),
        grid=(rows // tm,),
        in_specs=[_row_spec(tm, d), _const_spec((d, n))],
        out_specs=[_row_spec(tm, n) for _ in dtypes],
        out_shape=[jax.ShapeDtypeStruct((rows, n), dt) for dt in dtypes],
        compiler_params=_params(("parallel",), 40),
        name="proj",
    )(h, w)
    return outs


def _proj_rope_kernel(h_ref, w_ref, cos_ref, sin_ref, o_ref, *, scale, head_dim):
    z = _dot(h_ref[...], w_ref[...])
    cos = cos_ref[...]
    sin = sin_ref[...]
    for hd in range(z.shape[1] // head_dim):
        zh = z[:, hd * head_dim:(hd + 1) * head_dim]
        rot = zh * cos + pltpu.roll(zh, head_dim // 2, 1) * sin
        o_ref[:, hd * head_dim:(hd + 1) * head_dim] = (rot * scale).astype(o_ref.dtype)


def _proj_rope(h, w, cos2, sin2, pos_tiles, tm, scale, head_dim):
    rows, d = h.shape
    n = w.shape[1]
    tab = pl.BlockSpec((tm, head_dim), lambda i: (i % pos_tiles, 0))
    return pl.pallas_call(
        functools.partial(_proj_rope_kernel, scale=scale, head_dim=head_dim),
        grid=(rows // tm,),
        in_specs=[_row_spec(tm, d), _const_spec((d, n)), tab, tab],
        out_specs=_row_spec(tm, n),
        out_shape=jax.ShapeDtypeStruct((rows, n), BF16),
        compiler_params=_params(("parallel",), 40),
        name="proj_rope",
    )(h, w, cos2, sin2)


def _log_sigmoid(x):
    return jnp.minimum(x, 0.0) - jnp.log1p(jnp.exp(-jnp.abs(x)))


def _proj_logf_kernel(h_ref, w_ref, b_ref, o_ref):
    z = _dot(h_ref[...], w_ref[...]) + b_ref[...]
    o_ref[...] = _log_sigmoid(z)[:, :o_ref.shape[1]]


def _proj_logf(h, w_pad, b_pad, heads, tm):
    rows, d = h.shape
    n = w_pad.shape[1]
    return pl.pallas_call(
        _proj_logf_kernel,
        grid=(rows // tm,),
        in_specs=[_row_spec(tm, d), _const_spec((d, n)), _const_spec((1, n))],
        out_specs=_row_spec(tm, heads),
        out_shape=jax.ShapeDtypeStruct((rows, heads), F32),
        compiler_params=_params(("parallel",), 32),
        name="proj_logf",
    )(h, w_pad, b_pad)


def _cumsum_kernel(lf_ref, tri_ref, o_ref, carry_ref):
    @pl.when(pl.program_id(1) == 0)
    def _():
        carry_ref[...] = jnp.zeros_like(carry_ref)

    hi, lo = _split_bf16(lf_ref[...])
    tri = tri_ref[...]
    f = _dot(tri, hi) + _dot(tri, lo) + carry_ref[...]
    o_ref[...] = f
    carry_ref[...] = f[f.shape[0] - 1:, :]


def _cumsum_seq(lf, batch, tc=512):
    rows, heads = lf.shape
    nt = rows // batch // tc
    tri = (jnp.arange(tc)[:, None] >= jnp.arange(tc)[None, :]).astype(BF16)
    return pl.pallas_call(
        _cumsum_kernel,
        grid=(batch, nt),
        in_specs=[pl.BlockSpec((tc, heads), lambda b, t: (b * nt + t, 0)), _const_spec((tc, tc))],
        out_specs=pl.BlockSpec((tc, heads), lambda b, t: (b * nt + t, 0)),
        out_shape=jax.ShapeDtypeStruct((rows, heads), F32),
        scratch_shapes=[pltpu.VMEM((1, heads), F32)],
        compiler_params=_params(("parallel", "arbitrary"), 32),
        name="cumsum_logf",
    )(lf, tri)


LOG2E = 1.4426950408889634
_BIAS_PARTS = 3
_M_INIT = -1e28


def _split3(x):
    hi = x.astype(BF16).astype(F32)
    rest = x - hi
    mid = rest.astype(BF16).astype(F32)
    lo = (rest - mid).astype(BF16).astype(F32)
    return hi, mid, lo


def _fox_prompt_kernel(q_ref, k_ref, v_ref, f_ref, o_ref, kaug_ref, vt_ref, qt_ref, m_ref, acc_ref,
                       s_ref, p_ref, alpha_ref, d_ref, *, tq, tk, head_dim):
    qi = pl.program_id(2)
    nk = kaug_ref.shape[1]
    base = [head_dim * (1 - hd) for hd in range(2)]

    def own(lane, hd):
        return (lane >= hd * head_dim) & (lane < (hd + 1) * head_dim)

    def place(lane, parts_a, parts_b, hd):
        out = jnp.zeros(lane.shape, F32)
        for i, part in enumerate(list(parts_a) + list(parts_b)):
            out = jnp.where(lane == base[hd] + i, part, out)
        return out

    @pl.when(qi == 0)
    def _():
        ones = (1.0,) * _BIAS_PARTS
        lane = lax.broadcasted_iota(jnp.int32, (tk, LANES), 1)

        def prep(c, carry):
            start = pl.multiple_of(c * tk, tk)
            kc = k_ref[pl.ds(start, tk), :]
            vc = v_ref[pl.ds(start, tk), :].astype(F32)
            fc = f_ref[pl.ds(start, tk), :] * (-LOG2E)
            for hd in range(2):
                aug = place(lane, _split3(fc[:, hd:hd + 1]), ones, hd)
                kaug_ref[hd, c] = jnp.where(own(lane, hd), kc, aug.astype(BF16))
                v_aug = jnp.where(own(lane, hd), vc, jnp.where(lane == base[hd], 1.0, 0.0))
                vt_ref[hd, c] = v_aug.T.astype(BF16)
            return carry

        lax.fori_loop(0, nk, prep, 0)
        d_ref[...] = (lax.broadcasted_iota(jnp.int32, (tk, tq), 0)
                      - lax.broadcasted_iota(jnp.int32, (tk, tq), 1))

    lane_q = lax.broadcasted_iota(jnp.int32, (tq, LANES), 1)
    q = q_ref[...].astype(F32)
    f_first = f_ref[pl.ds(pl.multiple_of(qi * tq, tq), 1), :] * LOG2E
    for hd in range(2):
        aug = place(lane_q, (1.0,) * _BIAS_PARTS, _split3(f_first[:, hd:hd + 1]), hd)
        qt_ref[hd] = jnp.where(own(lane_q, hd), q, aug).T.astype(BF16)
    m_ref[...] = jnp.full_like(m_ref, _M_INIT)
    acc_ref[...] = jnp.zeros_like(acc_ref)
    s_ref[...] = jnp.full_like(s_ref, NEG)
    p_ref[...] = jnp.zeros_like(p_ref)
    alpha_ref[...] = jnp.ones_like(alpha_ref)

    def scores(t):
        j = jnp.minimum(t, nk - 1)
        keep = d_ref[...] <= qi * tq - t * tk
        for hd in range(2):
            s_t = _dot(kaug_ref[hd, j], qt_ref[hd])
            s_ref[hd] = jnp.where(keep, s_t, NEG)

    def softmax():
        for hd in range(2):
            s_t = s_ref[hd]
            m_prev = m_ref[hd]
            m_new = jnp.maximum(m_prev, jnp.max(s_t, axis=0, keepdims=True))
            alpha_ref[hd] = jnp.exp2(m_prev - m_new)
            p_ref[hd] = jnp.exp2(s_t - m_new).astype(BF16)
            m_ref[hd] = m_new

    def accumulate(t):
        j = jnp.clip(t, 0, nk - 1)
        for hd in range(2):
            acc_ref[hd] = alpha_ref[hd] * acc_ref[hd] + _dot(vt_ref[hd, j], p_ref[hd])

    def body(t, carry):
        accumulate(t - 2)
        softmax()
        scores(t)
        return carry

    causal_blocks = (qi + 1) * (tq // tk)
    lax.fori_loop(0, causal_blocks + 2, body, 0)

    outs = []
    for hd in range(2):
        acc = acc_ref[hd]
        outs.append((acc / acc[base[hd]:base[hd] + 1, :]).T)
    o_ref[...] = jnp.where(own(lane_q, 0), outs[0], outs[1]).astype(o_ref.dtype)


def _fox_prompt(q, k, v, f_cum, batch, seq, heads, head_dim, tq=512, tk=256):
    assert 2 * head_dim == LANES and 2 * _BIAS_PARTS <= head_dim and tq % tk == 0
    pairs = heads // 2
    nq = seq // tq
    nk = seq // tk
    width = heads * head_dim
    q3, k3, v3 = (t.reshape(batch, seq, width) for t in (q, k, v))
    f_pairs = f_cum.reshape(batch, seq, pairs, 2).transpose(0, 2, 1, 3)
    out = pl.pallas_call(
        functools.partial(_fox_prompt_kernel, tq=tq, tk=tk, head_dim=head_dim),
        grid=(batch, pairs, nq),
        in_specs=[
            pl.BlockSpec((None, tq, LANES), lambda b, p, i: (b, i, p)),
            pl.BlockSpec((None, seq, LANES), lambda b, p, i: (b, 0, p)),
            pl.BlockSpec((None, seq, LANES), lambda b, p, i: (b, 0, p)),
            pl.BlockSpec((None, None, seq, 2), lambda b, p, i: (b, p, 0, 0)),
        ],
        out_specs=pl.BlockSpec((None, tq, LANES), lambda b, p, i: (b, i, p)),
        out_shape=jax.ShapeDtypeStruct((batch, seq, width), BF16),
        scratch_shapes=[pltpu.VMEM((2, nk, tk, LANES), BF16), pltpu.VMEM((2, nk, LANES, tk), BF16),
                        pltpu.VMEM((2, LANES, tq), BF16), pltpu.VMEM((2, 1, tq), F32),
                        pltpu.VMEM((2, LANES, tq), F32), pltpu.VMEM((2, tk, tq), F32),
                        pltpu.VMEM((2, tk, tq), BF16), pltpu.VMEM((2, 1, tq), F32),
                        pltpu.VMEM((tk, tq), jnp.int32)],
        compiler_params=_params(("parallel", "parallel", "arbitrary"), 32),
        name="fox_prompt",
    )(q3, k3, v3, f_pairs)
    return out.reshape(batch * seq, width)


def _retention_kernel(q_ref, k_ref, v_ref, g_ref, s0_ref, dec_ref, qd_ref, kd_ref, gc_ref, gnw_ref,
                      r_ref, sfin_ref, st_ref, *, heads, dk, dv):
    c = pl.program_id(1)

    @pl.when(c == 0)
    def _():
        st_ref[...] = s0_ref[...]

    for hd in range(heads):
        ksl = slice(hd * dk, (hd + 1) * dk)
        vsl = slice(hd * dv, (hd + 1) * dv)
        q = q_ref[:, ksl]
        k = k_ref[:, ksl]
        v = v_ref[:, vsl]
        st = st_ref[hd]
        inner = _nt_dot(q, k) * dec_ref[hd]
        o = _dot(inner.astype(BF16), v) + _dot(q, st.astype(BF16)) * qd_ref[hd]
        k_dec = (k.astype(F32) * kd_ref[hd]).astype(BF16)
        st_ref[hd] = gc_ref[hd] * st + _tn_dot(k_dec, v)
        mu = jnp.mean(o, axis=-1, keepdims=True)
        cen = o - mu
        var = jnp.mean(cen * cen, axis=-1, keepdims=True)
        r = cen * lax.rsqrt(var + GN_EPS) * gnw_ref[:, vsl] * _silu(g_ref[:, vsl].astype(F32))
        r_ref[:, vsl] = r.astype(r_ref.dtype)

    @pl.when(c == pl.num_programs(1) - 1)
    def _():
        sfin_ref[...] = st_ref[...]


def _retention(q, k, v, g, state0, tables, gn_w, chunk):
    batch, seq, _ = q.shape
    _, heads, dk, dv = state0.shape
    dec, qd, kd, gc = tables
    nc = seq // chunk
    tok = lambda w: pl.BlockSpec((None, chunk, w), lambda b, c: (b, c, 0))
    st_spec = pl.BlockSpec((None, heads, dk, dv), lambda b, c: (b, 0, 0, 0))
    return pl.pallas_call(
        functools.partial(_retention_kernel, heads=heads, dk=dk, dv=dv),
        grid=(batch, nc),
        in_specs=[tok(heads * dk), tok(heads * dk), tok(heads * dv), tok(heads * dv), st_spec,
                  _const_spec(dec.shape), _const_spec(qd.shape), _const_spec(kd.shape),
                  _const_spec(gc.shape), _const_spec((1, heads * dv))],
        out_specs=[tok(heads * dv), st_spec],
        out_shape=[jax.ShapeDtypeStruct((batch, seq, heads * dv), BF16),
                   jax.ShapeDtypeStruct(state0.shape, F32)],
        scratch_shapes=[pltpu.VMEM((heads, dk, dv), F32)],
        compiler_params=_params(("parallel", "arbitrary"), 32),
        name="retention",
    )(q, k, v, g, state0, dec, qd, kd, gc, gn_w.reshape(1, heads * dv))


def _retention_tables(heads, chunk, padded):
    log_gamma = jnp.log(1.0 - 2.0 ** (-5.0 - jnp.arange(heads, dtype=F32)))
    i = jnp.arange(chunk, dtype=F32)
    diff = i[:, None] - i[None, :]
    dec = jnp.where(diff >= 0, jnp.exp(log_gamma[:, None, None] * jnp.maximum(diff, 0.0)), 0.0)
    qd = jnp.exp(log_gamma[:, None] * (i[None, :] + 1.0))[:, :, None]
    kd = jnp.exp(log_gamma[:, None] * (chunk - 1.0 - i[None, :]))[:, :, None]
    gc = jnp.exp(log_gamma * chunk)[:, None, None]
    pad = padded - chunk
    dec = jnp.pad(dec, ((0, 0), (0, pad), (0, pad)))
    qd = jnp.pad(qd, ((0, 0), (0, pad), (0, 0)))
    kd = jnp.pad(kd, ((0, 0), (0, pad), (0, 0)))
    return dec, qd, kd, gc


def _fox_sample_kernel(pt_ref, q_ref, kn_ref, vn_ref, lfn_ref, ueye_ref, usuf_ref, unew_ref,
                       mpast_ref, mnew_ref, *rest, pages_per_step, heads, new_tokens):
    g_n = pages_per_step
    k_refs = rest[:g_n]
    v_refs = rest[g_n:2 * g_n]
    lf_refs = rest[2 * g_n:3 * g_n]
    o_ref, m_ref, l_ref, acc_ref, carry_ref, lt_ref, c_ref, s_ref, p_ref = rest[3 * g_n:]
    j = pl.program_id(1)
    q = q_ref[...]
    page = lfn_ref.shape[0]

    def rows_of(x_ref):
        x = x_ref[...]
        return x.reshape(x.shape[0] * x.shape[1], x.shape[2]).astype(BF16)

    def update(pages, u_ref, mask_ref):
        carry = carry_ref[...]
        for i, (_, _, lf_ref) in enumerate(pages):
            hi, lo = _split_bf16(lf_ref[...])
            t_hi = _tn_dot(hi, ueye_ref[...])
            t_lo = _tn_dot(lo, ueye_ref[...])
            for t in range(new_tokens):
                rows = slice(t * heads, (t + 1) * heads)
                lt_ref[i, rows, :page] = t_hi[:, :page].astype(BF16)
                lt_ref[i, rows, page:] = t_lo[:, :page].astype(BF16)
                c_ref[i, rows, :] = carry
            carry = carry + t_hi[:, page:page + 1] + t_lo[:, page:page + 1]
        carry_ref[...] = carry
        for i, (k_ref, _, _) in enumerate(pages):
            s_ref[i] = _nt_dot(q, rows_of(k_ref)) + _dot(lt_ref[i], u_ref[...]) + mask_ref[...]
        m_prev = m_ref[...]
        m_new = m_prev
        for i in range(len(pages)):
            m_new = jnp.maximum(m_new, jnp.max(s_ref[i], axis=1, keepdims=True) + c_ref[i])
        alpha = jnp.exp(m_prev - m_new)
        l_new = alpha * l_ref[...]
        for i in range(len(pages)):
            p = jnp.exp(s_ref[i] - (m_new - c_ref[i]))
            l_new = l_new + jnp.sum(p, axis=1, keepdims=True)
            p_ref[i] = p.astype(BF16)
        acc = alpha * acc_ref[...]
        for i, (_, v_ref, _) in enumerate(pages):
            acc = acc + _dot(p_ref[i], rows_of(v_ref))
        l_ref[...] = l_new
        acc_ref[...] = acc
        m_ref[...] = m_new

    @pl.when(j == 0)
    def _():
        m_ref[...] = jnp.full_like(m_ref, NEG)
        l_ref[...] = jnp.zeros_like(l_ref)
        acc_ref[...] = jnp.zeros_like(acc_ref)
        carry_ref[...] = jnp.zeros_like(carry_ref)
        update([(kn_ref, vn_ref, lfn_ref)], unew_ref, mnew_ref)
        carry_ref[...] = jnp.zeros_like(carry_ref)

    update(list(zip(k_refs, v_refs, lf_refs)), usuf_ref, mpast_ref)

    @pl.when(j == pl.num_programs(1) - 1)
    def _():
        o_ref[...] = acc_ref[...] / l_ref[...]


def _fox_sample(q, k_new, v_new, lf_new, cache_k, cache_v, cache_lf, page_table, pages_per_step=4):
    batch, new_tokens, heads, head_dim = q.shape
    n_pages = page_table.shape[1]
    page = cache_k.shape[1]
    rows = new_tokens * heads
    cols = page * heads
    g_n = pages_per_step
    assert n_pages % g_n == 0 and new_tokens <= page

    pad_t = ((0, 0), (0, page - new_tokens), (0, 0), (0, 0))
    k_new = jnp.pad(k_new, pad_t)
    v_new = jnp.pad(v_new, pad_t)
    lf_new = jnp.pad(lf_new, pad_t[:3])

    src = jnp.arange(page)[:, None]
    col_key = (jnp.arange(cols) // heads)[None, :]
    col_head = (jnp.arange(cols) % heads)[None, :]
    row_t = (jnp.arange(rows) // heads)[:, None]
    row_head = (jnp.arange(rows) % heads)[:, None]
    ueye = jnp.concatenate([jnp.eye(page, dtype=F32), jnp.ones((page, 1), F32),
                            jnp.zeros((page, page - 1), F32)], axis=1).astype(BF16)
    twice = lambda u: jnp.concatenate([u, u], axis=0).astype(BF16)
    usuf = twice((src > col_key).astype(F32))
    unew = twice(-(src <= col_key).astype(F32))
    same = row_head == col_head
    mpast = jnp.where(same, 0.0, NEG).astype(F32)
    mnew = jnp.where(same & (col_key <= row_t), 0.0, NEG).astype(F32)

    def page_map(g):
        return lambda b, j, pt: (pt[b, n_pages - 1 - (j * g_n + g)], 0, 0, 0)

    const = lambda arr: pl.BlockSpec(arr.shape, lambda b, j, pt: (0,) * arr.ndim)
    kv_page = lambda imap: pl.BlockSpec((None, page, heads, head_dim), imap)
    per_b4 = lambda b, j, pt: (b, 0, 0, 0)
    in_specs = [pl.BlockSpec((None, rows, head_dim), lambda b, j, pt: (b, 0, 0)),
                kv_page(per_b4), kv_page(per_b4),
                pl.BlockSpec((None, page, heads), lambda b, j, pt: (b, 0, 0)),
                const(ueye), const(usuf), const(unew), const(mpast), const(mnew)]
    in_specs += [kv_page(page_map(g)) for g in range(g_n)]
    in_specs += [kv_page(page_map(g)) for g in range(g_n)]
    in_specs += [pl.BlockSpec((None, page, heads), lambda b, j, pt, g=g: page_map(g)(b, j, pt)[:3])
                 for g in range(g_n)]
    grid_spec = pltpu.PrefetchScalarGridSpec(
        num_scalar_prefetch=1,
        grid=(batch, n_pages // g_n),
        in_specs=in_specs,
        out_specs=pl.BlockSpec((None, rows, head_dim), lambda b, j, pt: (b, 0, 0)),
        scratch_shapes=[pltpu.VMEM((rows, 1), F32), pltpu.VMEM((rows, 1), F32),
                        pltpu.VMEM((rows, head_dim), F32), pltpu.VMEM((heads, 1), F32),
                        pltpu.VMEM((g_n, rows, 2 * page), BF16), pltpu.VMEM((g_n, rows, 1), F32),
                        pltpu.VMEM((g_n, rows, cols), F32), pltpu.VMEM((g_n, rows, cols), BF16)],
    )
    return pl.pallas_call(
        functools.partial(_fox_sample_kernel, pages_per_step=g_n, heads=heads, new_tokens=new_tokens),
        grid_spec=grid_spec,
        out_shape=jax.ShapeDtypeStruct((batch, rows, head_dim), F32),
        compiler_params=_params(("parallel", "arbitrary"), 56),
        name="fox_sample",
    )(page_table, q.reshape(batch, rows, head_dim), k_new, v_new, lf_new, ueye, usuf, unew, mpast, mnew,
      *([cache_k] * g_n), *([cache_v] * g_n), *([cache_lf] * g_n))


def _mix_out_kernel(a_ref, r_ref, ga_ref, gb_ref, x_ref, g1_ref, wo_ref, o_ref):
    m = (_sigmoid(ga_ref[...].astype(F32)) * a_ref[...].astype(F32)
         + _sigmoid(gb_ref[...].astype(F32)) * r_ref[...].astype(F32))
    o_ref[...] = x_ref[...] + g1_ref[...] * _dot(m.astype(BF16), wo_ref[...])


def _mix_out(a, r, ga, gb, x, g1, wo, rows_per_batch, tm):
    rows, d = x.shape
    w = a.shape[1]
    return pl.pallas_call(
        _mix_out_kernel,
        grid=(rows // tm,),
        in_specs=[_row_spec(tm, w), _row_spec(tm, w), _row_spec(tm, w), _row_spec(tm, w),
                  _row_spec(tm, d), _mod_spec(g1, tm, rows_per_batch), _const_spec(wo.shape)],
        out_specs=_row_spec(tm, d),
        out_shape=jax.ShapeDtypeStruct((rows, d), F32),
        compiler_params=_params(("parallel",), 48),
        name="mix_out",
    )(a, r, ga, gb, x, g1, wo)


def _ffn_kernel(x_ref, nw_ref, sc_ref, sh_ref, g2_ref, wg_ref, wu_ref, wd_ref, fw_ref, o_ref, acc_ref,
                *, chunk, final_norm):
    x = x_ref[...]
    h = (_rms(x, nw_ref[...]) * (1.0 + sc_ref[...]) + sh_ref[...]).astype(BF16)
    d_ff = wg_ref.shape[1]
    for c in range(d_ff // chunk):
        sl = slice(c * chunk, (c + 1) * chunk)
        gate = _dot(h, wg_ref[:, sl])
        up = _dot(h, wu_ref[:, sl])
        part = _dot((_silu(gate) * up).astype(BF16), wd_ref[sl, :])
        if c == 0:
            acc_ref[...] = part
        else:
            acc_ref[...] += part
    x2 = x + g2_ref[...] * acc_ref[...]
    o_ref[...] = _rms(x2, fw_ref[...]) if final_norm else x2


def _ffn(x, nw, sc, sh, g2, wg, wu, wd, fw, final_norm, rows_per_batch, tm, chunk=256):
    rows, d = x.shape
    d_ff = wg.shape[1]
    assert d_ff % chunk == 0
    once = lambda shape: pl.BlockSpec(shape, lambda *_: (0,) * len(shape), pipeline_mode=pl.Buffered(1))
    return pl.pallas_call(
        functools.partial(_ffn_kernel, chunk=chunk, final_norm=final_norm),
        grid=(rows // tm,),
        in_specs=[_row_spec(tm, d), _const_spec((1, d)), _mod_spec(sc, tm, rows_per_batch),
                  _mod_spec(sh, tm, rows_per_batch), _mod_spec(g2, tm, rows_per_batch),
                  once((d, d_ff)), once((d, d_ff)), once((d_ff, d)), _const_spec((1, d))],
        out_specs=_row_spec(tm, d),
        out_shape=jax.ShapeDtypeStruct((rows, d), F32),
        scratch_shapes=[pltpu.VMEM((tm, d), F32)],
        compiler_params=_params(("parallel",), 52),
        name="ffn",
    )(x, nw.reshape(1, d), sc, sh, g2, wg, wu, wd, fw.reshape(1, d))


def _rope_tables(pos, head_dim):
    half = head_dim // 2
    inv = ROPE_BASE ** (-jnp.arange(half, dtype=F32) / half)
    ang = pos.astype(F32)[:, None] * inv[None, :]
    cos, sin = jnp.cos(ang), jnp.sin(ang)
    return jnp.concatenate([cos, cos], axis=-1), jnp.concatenate([-sin, sin], axis=-1)


def _group_layer(x, rows_per_batch, mods, rope_tabs, pos_tiles, tm, tm_proj, lw, dims, fq_scale, mixer):
    sh1, sc1, g1, sh2, sc2, g2 = mods
    fox_heads, fox_dim, ret_heads, ret_dk, ret_dv = dims
    cos2, sin2 = rope_tabs
    h = _norm_mod(x, lw["attn_norm_w"], sc1, sh1, rows_per_batch, tm)
    (fq,) = _proj(h, lw["w_fq"], (BF16,), tm_proj, scale=fq_scale)
    fk32, fk16 = _proj(h, lw["w_fk"], (F32, BF16), tm_proj)
    fv32, fv16 = _proj(h, lw["w_fv"], (F32, BF16), tm_proj)
    logf = _proj_logf(h, lw["w_f"], lw["b_f"], fox_heads, tm_proj)
    rq = _proj_rope(h, lw["w_rq"], cos2, sin2, pos_tiles, tm_proj, ret_dk ** -0.5, ret_dk)
    rk = _proj_rope(h, lw["w_rk"], cos2, sin2, pos_tiles, tm_proj, 1.0, ret_dk)
    (rv,) = _proj(h, lw["w_rv"], (BF16,), tm_proj)
    (rg,) = _proj(h, lw["w_rg"], (F32,), tm_proj)
    (ga,) = _proj(h, lw["w_ga"], (F32,), tm_proj)
    (gb,) = _proj(h, lw["w_gb"], (F32,), tm_proj)
    a, r, s_fin = mixer(fq, fk16, fv16, logf, rq, rk, rv, rg)
    x1 = _mix_out(a, r, ga, gb, x, g1, lw["w_o"], rows_per_batch, tm)
    return x1, (fk32, fv32, logf, s_fin)


def kernel(x_prompt, x_sample, cache_k, cache_v, cache_logf, state_ret, page_table, c_prompt, c_sample,
           w_ada, b_ada, attn_norm_w, w_in, b_f, ret_gn_w, w_o, ffn_norm_w, w_gate, w_up, w_down,
           final_norm_w):
    batch, seq, d = x_prompt.shape
    dec_batch, dec_seq, _ = x_sample.shape
    depth, n_phys, page, fox_heads, fox_dim = cache_k.shape
    _, _, ret_heads, ret_dk, ret_dv = state_ret.shape
    n_pages = page_table.shape[1]
    past_len = n_pages * page
    w_fox = fox_heads * fox_dim
    w_rk = ret_heads * ret_dk
    w_rv = ret_heads * ret_dv
    dims = (fox_heads, fox_dim, ret_heads, ret_dk, ret_dv)
    splits = (w_fox, w_fox, w_fox, fox_heads, w_rk, w_rk, w_rv, w_rv, d, d)
    offs = np.concatenate([[0], np.cumsum(splits)])
    seg_names = ("w_fq", "w_fk", "w_fv", "w_f", "w_rq", "w_rk", "w_rv", "w_rg", "w_ga", "w_gb")

    xp = x_prompt.reshape(batch * seq, d)
    xs = x_sample.reshape(dec_batch * dec_seq, d)
    rows_s = dec_batch * dec_seq

    tm_p, tm_proj_p = 512, 1024
    rope_p = _rope_tables(jnp.arange(seq, dtype=jnp.int32), ret_dk)
    pos_s = past_len + jnp.arange(dec_seq, dtype=jnp.int32)
    rope_s = tuple(jnp.tile(t, (dec_batch, 1)) for t in _rope_tables(pos_s, ret_dk))
    tab_p = _retention_tables(ret_heads, RET_CHUNK, RET_CHUNK)
    dec_pad = 16
    tab_s = _retention_tables(ret_heads, dec_seq, dec_pad)

    ck = cache_k.reshape(depth * n_phys, page, fox_heads, fox_dim)
    cv = cache_v.reshape(depth * n_phys, page, fox_heads, fox_dim)
    clf = cache_logf.reshape(depth * n_phys, page, fox_heads)

    c_all = jnp.concatenate([c_prompt, c_sample], axis=0)
    c_rows = -(-c_all.shape[0] // 16) * 16
    c_all = jnp.pad(c_all, ((0, c_rows - c_all.shape[0]), (0, 0)))

    new_p, new_s = [], []
    for l in range(depth):
        mod = _ada(c_all, w_ada[l], b_ada[l])
        mods_p = tuple(m.reshape(batch, 1, d) for m in jnp.split(mod[:batch], 6, axis=-1))
        mods_s = tuple(jnp.repeat(m, dec_seq, axis=0)
                       for m in jnp.split(mod[batch:batch + dec_batch], 6, axis=-1))
        lw = {n: w_in[l][:, offs[i]:offs[i + 1]].astype(BF16) for i, n in enumerate(seg_names)}
        lw["w_f"] = jnp.pad(lw["w_f"], ((0, 0), (0, LANES - fox_heads)))
        lw["b_f"] = jnp.pad(b_f[l].astype(F32), (0, LANES - fox_heads)).reshape(1, LANES)
        lw["attn_norm_w"] = attn_norm_w[l]
        lw["w_o"] = w_o[l].astype(BF16)
        wg, wu, wd = w_gate[l].astype(BF16), w_up[l].astype(BF16), w_down[l].astype(BF16)

        def prompt_mixer(fq, fk, fv, logf, rq, rk, rv, rg):
            f_cum = _cumsum_seq(logf, batch)
            a = _fox_prompt(fq, fk, fv, f_cum, batch, seq, fox_heads, fox_dim)
            to3 = lambda t: t.reshape(batch, seq, t.shape[-1])
            s0 = jnp.zeros((batch, ret_heads, ret_dk, ret_dv), F32)
            r, s_fin = _retention(to3(rq), to3(rk), to3(rv), to3(rg), s0, tab_p, ret_gn_w[l], RET_CHUNK)
            return a, r.reshape(batch * seq, w_rv), s_fin

        def sample_mixer(fq, fk, fv, logf, rq, rk, rv, rg):
            to3 = lambda t: t.reshape(dec_batch, dec_seq, t.shape[-1])
            page_ids = page_table + l * n_phys
            to4 = lambda t: t.reshape(dec_batch, dec_seq, fox_heads, fox_dim)
            a = _fox_sample(to4(fq), to4(fk), to4(fv), to3(logf), ck, cv, clf, page_ids)
            padt = lambda t: jnp.pad(to3(t), ((0, 0), (0, dec_pad - dec_seq), (0, 0)))
            r, s_new = _retention(padt(rq), padt(rk), padt(rv), padt(rg), state_ret[l], tab_s,
                                  ret_gn_w[l], dec_pad)
            return a.reshape(rows_s, w_fox), r[:, :dec_seq].reshape(rows_s, w_rv), s_new

        xp1, st_p = _group_layer(xp, seq, mods_p, rope_p, seq // tm_proj_p, tm_p, tm_proj_p, lw, dims,
                                 fox_dim ** -0.5 * LOG2E, prompt_mixer)
        xs1, st_s = _group_layer(xs, dec_seq, mods_s, rope_s, 1, rows_s, rows_s, lw, dims,
                                 fox_dim ** -0.5, sample_mixer)
        last = l == depth - 1
        xp = _ffn(xp1, ffn_norm_w[l], mods_p[4], mods_p[3], mods_p[5], wg, wu, wd, final_norm_w, last,
                  seq, tm_p)
        xs = _ffn(xs1, ffn_norm_w[l], mods_s[4], mods_s[3], mods_s[5], wg, wu, wd, final_norm_w, last,
                  dec_seq, rows_s)
        new_p.append(st_p)
        new_s.append(st_s)

    y_prompt = xp.reshape(batch, seq, d)
    y_sample = xs.reshape(dec_batch, dec_seq, d)
    kv_p = lambda t: t.reshape(batch, seq, fox_heads, fox_dim)
    kv_s = lambda t: t.reshape(dec_batch, dec_seq, fox_heads, fox_dim)
    k_prompt = jnp.stack([kv_p(s[0]) for s in new_p])
    v_prompt = jnp.stack([kv_p(s[1]) for s in new_p])
    logf_prompt = jnp.stack([s[2].reshape(batch, seq, fox_heads) for s in new_p])
    ret_prompt = jnp.stack([s[3] for s in new_p])
    k_sample = jnp.stack([kv_s(s[0]) for s in new_s])
    v_sample = jnp.stack([kv_s(s[1]) for s in new_s])
    logf_sample = jnp.stack([s[2].reshape(dec_batch, dec_seq, fox_heads) for s in new_s])
    ret_sample = jnp.stack([s[3] for s in new_s])
    return (y_prompt, y_sample, k_prompt, v_prompt, logf_prompt, ret_prompt, k_sample, v_sample,
            logf_sample, ret_sample)
```

```python
import functools

import jax
import jax.numpy as jnp
import numpy as np
from jax import lax
from jax.experimental import pallas as pl
from jax.experimental.pallas import tpu as pltpu

F32 = jnp.float32
BF16 = jnp.bfloat16

NORM_EPS = 1e-6
GN_EPS = 1e-5
ROPE_BASE = 10000.0
NEG = -1e30
RET_CHUNK = 128
LANES = 128
MIB = 1 << 20


def _params(semantics, vmem_mib):
    return pltpu.CompilerParams(dimension_semantics=semantics, vmem_limit_bytes=vmem_mib * MIB)


def _nt_dot(a, b):
    return lax.dot_general(a, b, (((1,), (1,)), ((), ())), preferred_element_type=F32)


def _tn_dot(a, b):
    return lax.dot_general(a, b, (((0,), (0,)), ((), ())), preferred_element_type=F32)


def _dot(a, b):
    return jnp.dot(a, b, preferred_element_type=F32)


def _sigmoid(x):
    return 1.0 / (1.0 + jnp.exp(-x))


def _silu(x):
    return x * _sigmoid(x)


def _rms(x, w):
    return x * lax.rsqrt(jnp.mean(x * x, axis=-1, keepdims=True) + NORM_EPS) * w


def _split_bf16(x):
    hi = x.astype(BF16)
    lo = (x - hi.astype(F32)).astype(BF16)
    return hi, lo


def _row_spec(tm, width):
    return pl.BlockSpec((tm, width), lambda i: (i, 0))


def _const_spec(shape):
    return pl.BlockSpec(shape, lambda *_: (0,) * len(shape))


def _mod_spec(arr, tm, rows_per_batch):
    if arr.ndim == 3:
        tiles = rows_per_batch // tm
        return pl.BlockSpec((None, 1, arr.shape[-1]), lambda i: (i // tiles, 0, 0))
    return _row_spec(tm, arr.shape[-1])


def _ada_kernel(c_ref, w_ref, b_ref, o_ref):
    a = _silu(c_ref[...]).astype(BF16)
    o_ref[...] = _dot(a, w_ref[...].astype(BF16)) + b_ref[...]


def _ada(c, w, b, tn=1536):
    rows, d = c.shape
    n = w.shape[1]
    return pl.pallas_call(
        _ada_kernel,
        grid=(n // tn,),
        in_specs=[_const_spec((rows, d)), pl.BlockSpec((d, tn), lambda j: (0, j)),
                  pl.BlockSpec((1, tn), lambda j: (0, j))],
        out_specs=pl.BlockSpec((rows, tn), lambda j: (0, j)),
        out_shape=jax.ShapeDtypeStruct((rows, n), F32),
        compiler_params=_params(("parallel",), 40),
        name="ada",
    )(c, w, b.reshape(1, n))


def _norm_mod_kernel(x_ref, nw_ref, sc_ref, sh_ref, o_ref):
    y = _rms(x_ref[...], nw_ref[...])
    o_ref[...] = (y * (1.0 + sc_ref[...]) + sh_ref[...]).astype(o_ref.dtype)


def _norm_mod(x, nw, sc, sh, rows_per_batch, tm):
    rows, d = x.shape
    return pl.pallas_call(
        _norm_mod_kernel,
        grid=(rows // tm,),
        in_specs=[_row_spec(tm, d), _const_spec((1, d)), _mod_spec(sc, tm, rows_per_batch),
                  _mod_spec(sh, tm, rows_per_batch)],
        out_specs=_row_spec(tm, d),
        out_shape=jax.ShapeDtypeStruct((rows, d), BF16),
        compiler_params=_params(("parallel",), 32),
        name="norm_mod",
    )(x, nw.reshape(1, d), sc, sh)


def _proj_kernel(h_ref, w_ref, *o_refs, scale):
    z = _dot(h_ref[...], w_ref[...])
    if scale != 1.0:
        z = z * scale
    for o_ref in o_refs:
        o_ref[...] = z.astype(o_ref.dtype)


def _proj(h, w, dtypes, tm, scale=1.0):
    rows, d = h.shape
    n = w.shape[1]
    outs = pl.pallas_call(
        functools.partial(_proj_kernel, scale=scale),
        grid=(rows // tm,),
        in_specs=[_row_spec(tm, d), _const_spec((d, n))],
        out_specs=[_row_spec(tm, n) for _ in dtypes],
        out_shape=[jax.ShapeDtypeStruct((rows, n), dt) for dt in dtypes],
        compiler_params=_params(("parallel",), 40),
        name="proj",
    )(h, w)
    return outs


def _proj_rope_kernel(h_ref, w_ref, cos_ref, sin_ref, o_ref, *, scale, head_dim):
    z = _dot(h_ref[...], w_ref[...])
    cos = cos_ref[...]
    sin = sin_ref[...]
    for hd in range(z.shape[1] // head_dim):
        zh = z[:, hd * head_dim:(hd + 1) * head_dim]
        rot = zh * cos + pltpu.roll(zh, head_dim // 2, 1) * sin
        o_ref[:, hd * head_dim:(hd + 1) * head_dim] = (rot * scale).astype(o_ref.dtype)


def _proj_rope(h, w, cos2, sin2, pos_tiles, tm, scale, head_dim):
    rows, d = h.shape
    n = w.shape[1]
    tab = pl.BlockSpec((tm, head_dim), lambda i: (i % pos_tiles, 0))
    return pl.pallas_call(
        functools.partial(_proj_rope_kernel, scale=scale, head_dim=head_dim),
        grid=(rows // tm,),
        in_specs=[_row_spec(tm, d), _const_spec((d, n)), tab, tab],
        out_specs=_row_spec(tm, n),
        out_shape=jax.ShapeDtypeStruct((rows, n), BF16),
        compiler_params=_params(("parallel",), 40),
        name="proj_rope",
    )(h, w, cos2, sin2)


def _log_sigmoid(x):
    return jnp.minimum(x, 0.0) - jnp.log1p(jnp.exp(-jnp.abs(x)))


def _proj_logf_kernel(h_ref, w_ref, b_ref, o_ref):
    z = _dot(h_ref[...], w_ref[...]) + b_ref[...]
    o_ref[...] = _log_sigmoid(z)[:, :o_ref.shape[1]]


def _proj_logf(h, w_pad, b_pad, heads, tm):
    rows, d = h.shape
    n = w_pad.shape[1]
    return pl.pallas_call(
        _proj_logf_kernel,
        grid=(rows // tm,),
        in_specs=[_row_spec(tm, d), _const_spec((d, n)), _const_spec((1, n))],
        out_specs=_row_spec(tm, heads),
        out_shape=jax.ShapeDtypeStruct((rows, heads), F32),
        compiler_params=_params(("parallel",), 32),
        name="proj_logf",
    )(h, w_pad, b_pad)


def _cumsum_kernel(lf_ref, tri_ref, o_ref, carry_ref):
    @pl.when(pl.program_id(1) == 0)
    def _():
        carry_ref[...] = jnp.zeros_like(carry_ref)

    hi, lo = _split_bf16(lf_ref[...])
    tri = tri_ref[...]
    f = _dot(tri, hi) + _dot(tri, lo) + carry_ref[...]
    o_ref[...] = f
    carry_ref[...] = f[f.shape[0] - 1:, :]


def _cumsum_seq(lf, batch, tc=512):
    rows, heads = lf.shape
    nt = rows // batch // tc
    tri = (jnp.arange(tc)[:, None] >= jnp.arange(tc)[None, :]).astype(BF16)
    return pl.pallas_call(
        _cumsum_kernel,
        grid=(batch, nt),
        in_specs=[pl.BlockSpec((tc, heads), lambda b, t: (b * nt + t, 0)), _const_spec((tc, tc))],
        out_specs=pl.BlockSpec((tc, heads), lambda b, t: (b * nt + t, 0)),
        out_shape=jax.ShapeDtypeStruct((rows, heads), F32),
        scratch_shapes=[pltpu.VMEM((1, heads), F32)],
        compiler_params=_params(("parallel", "arbitrary"), 32),
        name="cumsum_logf",
    )(lf, tri)


LOG2E = 1.4426950408889634
_BIAS_PARTS = 3
_M_INIT = -1e28


def _split3(x):
    hi = x.astype(BF16).astype(F32)
    rest = x - hi
    mid = rest.astype(BF16).astype(F32)
    lo = (rest - mid).astype(BF16).astype(F32)
    return hi, mid, lo


def _fox_prompt_kernel(q_ref, k_ref, v_ref, f_ref, o_ref, kaug_ref, vt_ref, qt_ref, m_ref, acc_ref,
                       s_ref, p_ref, alpha_ref, d_ref, *, tq, tk, head_dim):
    qi = pl.program_id(2)
    nk = kaug_ref.shape[1]
    base = [head_dim * (1 - hd) for hd in range(2)]

    def own(lane, hd):
        return (lane >= hd * head_dim) & (lane < (hd + 1) * head_dim)

    def place(lane, parts_a, parts_b, hd):
        out = jnp.zeros(lane.shape, F32)
        for i, part in enumerate(list(parts_a) + list(parts_b)):
            out = jnp.where(lane == base[hd] + i, part, out)
        return out

    @pl.when(qi == 0)
    def _():
        ones = (1.0,) * _BIAS_PARTS
        lane = lax.broadcasted_iota(jnp.int32, (tk, LANES), 1)

        def prep(c, carry):
            start = pl.multiple_of(c * tk, tk)
            kc = k_ref[pl.ds(start, tk), :]
            vc = v_ref[pl.ds(start, tk), :].astype(F32)
            fc = f_ref[pl.ds(start, tk), :] * (-LOG2E)
            for hd in range(2):
                aug = place(lane, _split3(fc[:, hd:hd + 1]), ones, hd)
                kaug_ref[hd, c] = jnp.where(own(lane, hd), kc, aug.astype(BF16))
                v_aug = jnp.where(own(lane, hd), vc, jnp.where(lane == base[hd], 1.0, 0.0))
                vt_ref[hd, c] = v_aug.T.astype(BF16)
            return carry

        lax.fori_loop(0, nk, prep, 0)
        d_ref[...] = (lax.broadcasted_iota(jnp.int32, (tk, tq), 0)
                      - lax.broadcasted_iota(jnp.int32, (tk, tq), 1))

    lane_q = lax.broadcasted_iota(jnp.int32, (tq, LANES), 1)
    q = q_ref[...].astype(F32)
    f_first = f_ref[pl.ds(pl.multiple_of(qi * tq, tq), 1), :] * LOG2E
    for hd in range(2):
        aug = place(lane_q, (1.0,) * _BIAS_PARTS, _split3(f_first[:, hd:hd + 1]), hd)
        qt_ref[hd] = jnp.where(own(lane_q, hd), q, aug).T.astype(BF16)
    m_ref[...] = jnp.full_like(m_ref, _M_INIT)
    acc_ref[...] = jnp.zeros_like(acc_ref)
    s_ref[...] = jnp.full_like(s_ref, NEG)
    p_ref[...] = jnp.zeros_like(p_ref)
    alpha_ref[...] = jnp.ones_like(alpha_ref)

    def scores(t):
        j = jnp.minimum(t, nk - 1)
        keep = d_ref[...] <= qi * tq - t * tk
        for hd in range(2):
            s_t = _dot(kaug_ref[hd, j], qt_ref[hd])
            s_ref[hd] = jnp.where(keep, s_t, NEG)

    def softmax():
        for hd in range(2):
            s_t = s_ref[hd]
            m_prev = m_ref[hd]
            m_new = jnp.maximum(m_prev, jnp.max(s_t, axis=0, keepdims=True))
            alpha_ref[hd] = jnp.exp2(m_prev - m_new)
            p_ref[hd] = jnp.exp2(s_t - m_new).astype(BF16)
            m_ref[hd] = m_new

    def accumulate(t):
        j = jnp.clip(t, 0, nk - 1)
        for hd in range(2):
            acc_ref[hd] = alpha_ref[hd] * acc_ref[hd] + _dot(vt_ref[hd, j], p_ref[hd])

    def body(t, carry):
        accumulate(t - 2)
        softmax()
        scores(t)
        return carry

    causal_blocks = (qi + 1) * (tq // tk)
    lax.fori_loop(0, causal_blocks + 2, body, 0)

    outs = []
    for hd in range(2):
        acc = acc_ref[hd]
        outs.append((acc / acc[base[hd]:base[hd] + 1, :]).T)
    o_ref[...] = jnp.where(own(lane_q, 0), outs[0], outs[1]).astype(o_ref.dtype)


def _fox_prompt(q, k, v, f_cum, batch, seq, heads, head_dim, tq=512, tk=256):
    assert 2 * head_dim == LANES and 2 * _BIAS_PARTS <= head_dim and tq % tk == 0
    pairs = heads // 2
    nq = seq // tq
    nk = seq // tk
    width = heads * head_dim
    q3, k3, v3 = (t.reshape(batch, seq, width) for t in (q, k, v))
    f_pairs = f_cum.reshape(batch, seq, pairs, 2).transpose(0, 2, 1, 3)
    out = pl.pallas_call(
        functools.partial(_fox_prompt_kernel, tq=tq, tk=tk, head_dim=head_dim),
        grid=(batch, pairs, nq),
        in_specs=[
            pl.BlockSpec((None, tq, LANES), lambda b, p, i: (b, i, p)),
            pl.BlockSpec((None, seq, LANES), lambda b, p, i: (b, 0, p)),
            pl.BlockSpec((None, seq, LANES), lambda b, p, i: (b, 0, p)),
            pl.BlockSpec((None, None, seq, 2), lambda b, p, i: (b, p, 0, 0)),
        ],
        out_specs=pl.BlockSpec((None, tq, LANES), lambda b, p, i: (b, i, p)),
        out_shape=jax.ShapeDtypeStruct((batch, seq, width), BF16),
        scratch_shapes=[pltpu.VMEM((2, nk, tk, LANES), BF16), pltpu.VMEM((2, nk, LANES, tk), BF16),
                        pltpu.VMEM((2, LANES, tq), BF16), pltpu.VMEM((2, 1, tq), F32),
                        pltpu.VMEM((2, LANES, tq), F32), pltpu.VMEM((2, tk, tq), F32),
                        pltpu.VMEM((2, tk, tq), BF16), pltpu.VMEM((2, 1, tq), F32),
                        pltpu.VMEM((tk, tq), jnp.int32)],
        compiler_params=_params(("parallel", "parallel", "arbitrary"), 32),
        name="fox_prompt",
    )(q3, k3, v3, f_pairs)
    return out.reshape(batch * seq, width)


def _retention_kernel(q_ref, k_ref, v_ref, g_ref, s0_ref, dec_ref, qd_ref, kd_ref, gc_ref, gnw_ref,
                      r_ref, sfin_ref, st_ref, *, heads, dk, dv):
    c = pl.program_id(1)

    @pl.when(c == 0)
    def _():
        st_ref[...] = s0_ref[...]

    for hd in range(heads):
        ksl = slice(hd * dk, (hd + 1) * dk)
        vsl = slice(hd * dv, (hd + 1) * dv)
        q = q_ref[:, ksl]
        k = k_ref[:, ksl]
        v = v_ref[:, vsl]
        st = st_ref[hd]
        inner = _nt_dot(q, k) * dec_ref[hd]
        o = _dot(inner.astype(BF16), v) + _dot(q, st.astype(BF16)) * qd_ref[hd]
        k_dec = (k.astype(F32) * kd_ref[hd]).astype(BF16)
        st_ref[hd] = gc_ref[hd] * st + _tn_dot(k_dec, v)
        mu = jnp.mean(o, axis=-1, keepdims=True)
        cen = o - mu
        var = jnp.mean(cen * cen, axis=-1, keepdims=True)
        r = cen * lax.rsqrt(var + GN_EPS) * gnw_ref[:, vsl] * _silu(g_ref[:, vsl].astype(F32))
        r_ref[:, vsl] = r.astype(r_ref.dtype)

    @pl.when(c == pl.num_programs(1) - 1)
    def _():
        sfin_ref[...] = st_ref[...]


def _retention(q, k, v, g, state0, tables, gn_w, chunk):
    batch, seq, _ = q.shape
    _, heads, dk, dv = state0.shape
    dec, qd, kd, gc = tables
    nc = seq // chunk
    tok = lambda w: pl.BlockSpec((None, chunk, w), lambda b, c: (b, c, 0))
    st_spec = pl.BlockSpec((None, heads, dk, dv), lambda b, c: (b, 0, 0, 0))
    return pl.pallas_call(
        functools.partial(_retention_kernel, heads=heads, dk=dk, dv=dv),
        grid=(batch, nc),
        in_specs=[tok(heads * dk), tok(heads * dk), tok(heads * dv), tok(heads * dv), st_spec,
                  _const_spec(dec.shape), _const_spec(qd.shape), _const_spec(kd.shape),
                  _const_spec(gc.shape), _const_spec((1, heads * dv))],
        out_specs=[tok(heads * dv), st_spec],
        out_shape=[jax.ShapeDtypeStruct((batch, seq, heads * dv), BF16),
                   jax.ShapeDtypeStruct(state0.shape, F32)],
        scratch_shapes=[pltpu.VMEM((heads, dk, dv), F32)],
        compiler_params=_params(("parallel", "arbitrary"), 32),
        name="retention",
    )(q, k, v, g, state0, dec, qd, kd, gc, gn_w.reshape(1, heads * dv))


def _retention_tables(heads, chunk, padded):
    log_gamma = jnp.log(1.0 - 2.0 ** (-5.0 - jnp.arange(heads, dtype=F32)))
    i = jnp.arange(chunk, dtype=F32)
    diff = i[:, None] - i[None, :]
    dec = jnp.where(diff >= 0, jnp.exp(log_gamma[:, None, None] * jnp.maximum(diff, 0.0)), 0.0)
    qd = jnp.exp(log_gamma[:, None] * (i[None, :] + 1.0))[:, :, None]
    kd = jnp.exp(log_gamma[:, None] * (chunk - 1.0 - i[None, :]))[:, :, None]
    gc = jnp.exp(log_gamma * chunk)[:, None, None]
    pad = padded - chunk
    dec = jnp.pad(dec, ((0, 0), (0, pad), (0, pad)))
    qd = jnp.pad(qd, ((0, 0), (0, pad), (0, 0)))
    kd = jnp.pad(kd, ((0, 0), (0, pad), (0, 0)))
    return dec, qd, kd, gc


def _fox_sample_kernel(pt_ref, qbd_ref, kn_ref, vn_ref, lfn_ref, usuf_ref, uinc_ref, *rest,
                       pages_per_step, heads, head_dim, new_tokens, page):
    g_n = pages_per_step
    k_refs = rest[:g_n]
    v_refs = rest[g_n:2 * g_n]
    lf_refs = rest[2 * g_n:3 * g_n]
    o_ref, m_ref, l_ref, acc_ref, carry_ref, bias_ref, kt_ref, vt_ref = rest[3 * g_n:]
    j = pl.program_id(1)
    rows = new_tokens * heads
    qbd = qbd_ref[...]

    def set_bias(g, per_head):
        for t in range(new_tokens):
            bias_ref[t * heads:(t + 1) * heads, g * page:(g + 1) * page] = per_head

    def update(s, weighted_values):
        m_prev = m_ref[...]
        m_new = jnp.maximum(m_prev, jnp.max(s, axis=1, keepdims=True))
        alpha = jnp.exp(m_prev - m_new)
        p = jnp.exp(s - m_new)
        l_ref[...] = alpha * l_ref[...] + jnp.sum(p, axis=1, keepdims=True)
        acc_ref[...] = alpha * acc_ref[...] + weighted_values(p.astype(BF16))
        m_ref[...] = m_new

    @pl.when(j == 0)
    def _():
        m_ref[...] = jnp.full_like(m_ref, NEG)
        l_ref[...] = jnp.zeros_like(l_ref)
        acc_ref[...] = jnp.zeros_like(acc_ref)
        carry_ref[...] = jnp.zeros_like(carry_ref)
        kn = kn_ref[...]
        zpad = jnp.zeros((page - kn.shape[0], kn.shape[1]), kn.dtype)
        kp = jnp.concatenate([kn, zpad], axis=0)
        vp = jnp.concatenate([vn_ref[...], zpad], axis=0)
        lfn = lfn_ref[...]
        lf = jnp.concatenate([lfn, jnp.zeros((page - lfn.shape[0], lfn.shape[1]), F32)], axis=0)
        hi, lo = _split_bf16(lf)
        set_bias(0, (_tn_dot(hi, uinc_ref[...]) + _tn_dot(lo, uinc_ref[...]))[:, :page])
        row = lax.broadcasted_iota(jnp.int32, (rows, page), 0)
        col = lax.broadcasted_iota(jnp.int32, (rows, page), 1)
        s = jnp.where(col * heads <= row, _nt_dot(qbd, kp) + bias_ref[:, :page], NEG)
        update(s, lambda p: _dot(p, vp))

    carry = carry_ref[...]
    for g in range(g_n):
        hi, lo = _split_bf16(lf_refs[g][...])
        b = _dot(hi, usuf_ref[...]) + _dot(lo, usuf_ref[...])
        set_bias(g, b[:, :page] + carry)
        carry = carry + b[:, page:page + 1]
        kt_ref[:, g * page:(g + 1) * page] = k_refs[g][...].reshape(heads * head_dim, page).astype(BF16)
        vt_ref[:, g * page:(g + 1) * page] = v_refs[g][...].reshape(heads * head_dim, page).astype(BF16)
    carry_ref[...] = carry
    s = _dot(qbd, kt_ref[...]) + bias_ref[...]
    update(s, lambda p: _nt_dot(p, vt_ref[...]))

    @pl.when(j == pl.num_programs(1) - 1)
    def _():
        out = acc_ref[...] / l_ref[...]
        row = lax.broadcasted_iota(jnp.int32, out.shape, 0)
        col = lax.broadcasted_iota(jnp.int32, out.shape, 1)
        out = jnp.where((row % heads) == (col // head_dim), out, 0.0)
        o_ref[...] = out.reshape(new_tokens, heads, out.shape[1]).sum(axis=1)


def _fox_sample(q, k_new, v_new, lf_new, cache_kt, cache_vt, cache_lft, page_table, pages_per_step=8):
    batch, new_tokens, width = q.shape
    _, heads, head_dim, page = cache_kt.shape
    n_pages = page_table.shape[1]
    rows = new_tokens * heads
    g_n = pages_per_step
    assert n_pages % g_n == 0 and page == LANES and width == heads * head_dim

    eye = jnp.eye(heads, dtype=q.dtype)
    qbd = (q.reshape(batch, new_tokens, heads, 1, head_dim) * eye[None, None, :, :, None]
           ).reshape(batch, rows, width)
    pad_t = 16 - new_tokens
    k_new = jnp.pad(k_new, ((0, 0), (0, pad_t), (0, 0)))
    v_new = jnp.pad(v_new, ((0, 0), (0, pad_t), (0, 0)))
    lf_new = jnp.pad(lf_new, ((0, 0), (0, 8 - new_tokens), (0, 0)))

    idx = jnp.arange(page)
    ones_col = (jnp.arange(page)[None, :] == 0).astype(F32) * jnp.ones((page, 1), F32)
    usuf = jnp.concatenate([(idx[:, None] > idx[None, :]).astype(F32), ones_col], axis=1).astype(BF16)
    uinc = jnp.concatenate([-(idx[:, None] <= idx[None, :]).astype(F32), ones_col], axis=1).astype(BF16)

    def page_map(g):
        return lambda b, j, pt: (pt[b, n_pages - 1 - (j * g_n + g)], 0, 0, 0)

    per_b = lambda shape: pl.BlockSpec((None,) + shape, lambda b, j, pt: (b, 0, 0))
    in_specs = [per_b((rows, width)), per_b((16, width)), per_b((16, width)), per_b((8, heads)),
                pl.BlockSpec(usuf.shape, lambda b, j, pt: (0, 0)),
                pl.BlockSpec(uinc.shape, lambda b, j, pt: (0, 0))]
    in_specs += [pl.BlockSpec((None, heads, head_dim, page), page_map(g)) for g in range(g_n)]
    in_specs += [pl.BlockSpec((None, heads, head_dim, page), page_map(g)) for g in range(g_n)]
    in_specs += [pl.BlockSpec((None, heads, page), lambda b, j, pt, g=g: page_map(g)(b, j, pt)[:3])
                 for g in range(g_n)]
    grid_spec = pltpu.PrefetchScalarGridSpec(
        num_scalar_prefetch=1,
        grid=(batch, n_pages // g_n),
        in_specs=in_specs,
        out_specs=pl.BlockSpec((None, new_tokens, width), lambda b, j, pt: (b, 0, 0)),
        scratch_shapes=[pltpu.VMEM((rows, 1), F32), pltpu.VMEM((rows, 1), F32),
                        pltpu.VMEM((rows, width), F32), pltpu.VMEM((heads, 1), F32),
                        pltpu.VMEM((rows, g_n * page), F32), pltpu.VMEM((width, g_n * page), BF16),
                        pltpu.VMEM((width, g_n * page), BF16)],
    )
    return pl.pallas_call(
        functools.partial(_fox_sample_kernel, pages_per_step=g_n, heads=heads, head_dim=head_dim,
                          new_tokens=new_tokens, page=page),
        grid_spec=grid_spec,
        out_shape=jax.ShapeDtypeStruct((batch, new_tokens, width), F32),
        compiler_params=_params(("parallel", "arbitrary"), 48),
        name="fox_sample",
    )(page_table, qbd, k_new, v_new, lf_new, usuf, uinc,
      *([cache_kt] * g_n), *([cache_vt] * g_n), *([cache_lft] * g_n))


def _mix_out_kernel(a_ref, r_ref, ga_ref, gb_ref, x_ref, g1_ref, wo_ref, o_ref):
    m = (_sigmoid(ga_ref[...].astype(F32)) * a_ref[...].astype(F32)
         + _sigmoid(gb_ref[...].astype(F32)) * r_ref[...].astype(F32))
    o_ref[...] = x_ref[...] + g1_ref[...] * _dot(m.astype(BF16), wo_ref[...])


def _mix_out(a, r, ga, gb, x, g1, wo, rows_per_batch, tm):
    rows, d = x.shape
    w = a.shape[1]
    return pl.pallas_call(
        _mix_out_kernel,
        grid=(rows // tm,),
        in_specs=[_row_spec(tm, w), _row_spec(tm, w), _row_spec(tm, w), _row_spec(tm, w),
                  _row_spec(tm, d), _mod_spec(g1, tm, rows_per_batch), _const_spec(wo.shape)],
        out_specs=_row_spec(tm, d),
        out_shape=jax.ShapeDtypeStruct((rows, d), F32),
        compiler_params=_params(("parallel",), 48),
        name="mix_out",
    )(a, r, ga, gb, x, g1, wo)


def _ffn_kernel(x_ref, nw_ref, sc_ref, sh_ref, g2_ref, wg_ref, wu_ref, wd_ref, fw_ref, o_ref, acc_ref,
                *, chunk, final_norm):
    x = x_ref[...]
    h = (_rms(x, nw_ref[...]) * (1.0 + sc_ref[...]) + sh_ref[...]).astype(BF16)
    d_ff = wg_ref.shape[1]
    for c in range(d_ff // chunk):
        sl = slice(c * chunk, (c + 1) * chunk)
        gate = _dot(h, wg_ref[:, sl])
        up = _dot(h, wu_ref[:, sl])
        part = _dot((_silu(gate) * up).astype(BF16), wd_ref[sl, :])
        if c == 0:
            acc_ref[...] = part
        else:
            acc_ref[...] += part
    x2 = x + g2_ref[...] * acc_ref[...]
    o_ref[...] = _rms(x2, fw_ref[...]) if final_norm else x2


def _ffn(x, nw, sc, sh, g2, wg, wu, wd, fw, final_norm, rows_per_batch, tm, chunk=256):
    rows, d = x.shape
    d_ff = wg.shape[1]
    assert d_ff % chunk == 0
    once = lambda shape: pl.BlockSpec(shape, lambda *_: (0,) * len(shape), pipeline_mode=pl.Buffered(1))
    return pl.pallas_call(
        functools.partial(_ffn_kernel, chunk=chunk, final_norm=final_norm),
        grid=(rows // tm,),
        in_specs=[_row_spec(tm, d), _const_spec((1, d)), _mod_spec(sc, tm, rows_per_batch),
                  _mod_spec(sh, tm, rows_per_batch), _mod_spec(g2, tm, rows_per_batch),
                  once((d, d_ff)), once((d, d_ff)), once((d_ff, d)), _const_spec((1, d))],
        out_specs=_row_spec(tm, d),
        out_shape=jax.ShapeDtypeStruct((rows, d), F32),
        scratch_shapes=[pltpu.VMEM((tm, d), F32)],
        compiler_params=_params(("parallel",), 52),
        name="ffn",
    )(x, nw.reshape(1, d), sc, sh, g2, wg, wu, wd, fw.reshape(1, d))


def _rope_tables(pos, head_dim):
    half = head_dim // 2
    inv = ROPE_BASE ** (-jnp.arange(half, dtype=F32) / half)
    ang = pos.astype(F32)[:, None] * inv[None, :]
    cos, sin = jnp.cos(ang), jnp.sin(ang)
    return jnp.concatenate([cos, cos], axis=-1), jnp.concatenate([-sin, sin], axis=-1)


def _group_layer(x, rows_per_batch, mods, rope_tabs, pos_tiles, tm, tm_proj, lw, dims, fq_scale, mixer):
    sh1, sc1, g1, sh2, sc2, g2 = mods
    fox_heads, fox_dim, ret_heads, ret_dk, ret_dv = dims
    cos2, sin2 = rope_tabs
    h = _norm_mod(x, lw["attn_norm_w"], sc1, sh1, rows_per_batch, tm)
    (fq,) = _proj(h, lw["w_fq"], (BF16,), tm_proj, scale=fq_scale)
    fk32, fk16 = _proj(h, lw["w_fk"], (F32, BF16), tm_proj)
    fv32, fv16 = _proj(h, lw["w_fv"], (F32, BF16), tm_proj)
    logf = _proj_logf(h, lw["w_f"], lw["b_f"], fox_heads, tm_proj)
    rq = _proj_rope(h, lw["w_rq"], cos2, sin2, pos_tiles, tm_proj, ret_dk ** -0.5, ret_dk)
    rk = _proj_rope(h, lw["w_rk"], cos2, sin2, pos_tiles, tm_proj, 1.0, ret_dk)
    (rv,) = _proj(h, lw["w_rv"], (BF16,), tm_proj)
    (rg,) = _proj(h, lw["w_rg"], (F32,), tm_proj)
    (ga,) = _proj(h, lw["w_ga"], (F32,), tm_proj)
    (gb,) = _proj(h, lw["w_gb"], (F32,), tm_proj)
    a, r, s_fin = mixer(fq, fk16, fv16, logf, rq, rk, rv, rg)
    x1 = _mix_out(a, r, ga, gb, x, g1, lw["w_o"], rows_per_batch, tm)
    return x1, (fk32, fv32, logf, s_fin)


def kernel(x_prompt, x_sample, cache_k, cache_v, cache_logf, state_ret, page_table, c_prompt, c_sample,
           w_ada, b_ada, attn_norm_w, w_in, b_f, ret_gn_w, w_o, ffn_norm_w, w_gate, w_up, w_down,
           final_norm_w):
    batch, seq, d = x_prompt.shape
    dec_batch, dec_seq, _ = x_sample.shape
    depth, n_phys, page, fox_heads, fox_dim = cache_k.shape
    _, _, ret_heads, ret_dk, ret_dv = state_ret.shape
    n_pages = page_table.shape[1]
    past_len = n_pages * page
    w_fox = fox_heads * fox_dim
    w_rk = ret_heads * ret_dk
    w_rv = ret_heads * ret_dv
    dims = (fox_heads, fox_dim, ret_heads, ret_dk, ret_dv)
    splits = (w_fox, w_fox, w_fox, fox_heads, w_rk, w_rk, w_rv, w_rv, d, d)
    offs = np.concatenate([[0], np.cumsum(splits)])
    seg_names = ("w_fq", "w_fk", "w_fv", "w_f", "w_rq", "w_rk", "w_rv", "w_rg", "w_ga", "w_gb")

    xp = x_prompt.reshape(batch * seq, d)
    xs = x_sample.reshape(dec_batch * dec_seq, d)
    rows_s = dec_batch * dec_seq

    tm_p, tm_proj_p = 512, 1024
    rope_p = _rope_tables(jnp.arange(seq, dtype=jnp.int32), ret_dk)
    pos_s = past_len + jnp.arange(dec_seq, dtype=jnp.int32)
    rope_s = tuple(jnp.tile(t, (dec_batch, 1)) for t in _rope_tables(pos_s, ret_dk))
    tab_p = _retention_tables(ret_heads, RET_CHUNK, RET_CHUNK)
    dec_pad = 16
    tab_s = _retention_tables(ret_heads, dec_seq, dec_pad)

    ck = jnp.transpose(cache_k, (0, 1, 3, 4, 2)).reshape(depth * n_phys, fox_heads, fox_dim, page)
    cv = jnp.transpose(cache_v, (0, 1, 3, 4, 2)).reshape(depth * n_phys, fox_heads, fox_dim, page)
    clf = jnp.transpose(cache_logf, (0, 1, 3, 2)).reshape(depth * n_phys, fox_heads, page)

    c_all = jnp.concatenate([c_prompt, c_sample], axis=0)
    c_rows = -(-c_all.shape[0] // 16) * 16
    c_all = jnp.pad(c_all, ((0, c_rows - c_all.shape[0]), (0, 0)))

    new_p, new_s = [], []
    for l in range(depth):
        mod = _ada(c_all, w_ada[l], b_ada[l])
        mods_p = tuple(m.reshape(batch, 1, d) for m in jnp.split(mod[:batch], 6, axis=-1))
        mods_s = tuple(jnp.repeat(m, dec_seq, axis=0)
                       for m in jnp.split(mod[batch:batch + dec_batch], 6, axis=-1))
        lw = {n: w_in[l][:, offs[i]:offs[i + 1]].astype(BF16) for i, n in enumerate(seg_names)}
        lw["w_f"] = jnp.pad(lw["w_f"], ((0, 0), (0, LANES - fox_heads)))
        lw["b_f"] = jnp.pad(b_f[l].astype(F32), (0, LANES - fox_heads)).reshape(1, LANES)
        lw["attn_norm_w"] = attn_norm_w[l]
        lw["w_o"] = w_o[l].astype(BF16)
        wg, wu, wd = w_gate[l].astype(BF16), w_up[l].astype(BF16), w_down[l].astype(BF16)

        def prompt_mixer(fq, fk, fv, logf, rq, rk, rv, rg):
            f_cum = _cumsum_seq(logf, batch)
            a = _fox_prompt(fq, fk, fv, f_cum, batch, seq, fox_heads, fox_dim)
            to3 = lambda t: t.reshape(batch, seq, t.shape[-1])
            s0 = jnp.zeros((batch, ret_heads, ret_dk, ret_dv), F32)
            r, s_fin = _retention(to3(rq), to3(rk), to3(rv), to3(rg), s0, tab_p, ret_gn_w[l], RET_CHUNK)
            return a, r.reshape(batch * seq, w_rv), s_fin

        def sample_mixer(fq, fk, fv, logf, rq, rk, rv, rg):
            to3 = lambda t: t.reshape(dec_batch, dec_seq, t.shape[-1])
            page_ids = page_table + l * n_phys
            a = _fox_sample(to3(fq), to3(fk), to3(fv), to3(logf), ck, cv, clf, page_ids)
            padt = lambda t: jnp.pad(to3(t), ((0, 0), (0, dec_pad - dec_seq), (0, 0)))
            r, s_new = _retention(padt(rq), padt(rk), padt(rv), padt(rg), state_ret[l], tab_s,
                                  ret_gn_w[l], dec_pad)
            return a.reshape(rows_s, w_fox), r[:, :dec_seq].reshape(rows_s, w_rv), s_new

        xp1, st_p = _group_layer(xp, seq, mods_p, rope_p, seq // tm_proj_p, tm_p, tm_proj_p, lw, dims,
                                 fox_dim ** -0.5 * LOG2E, prompt_mixer)
        xs1, st_s = _group_layer(xs, dec_seq, mods_s, rope_s, 1, rows_s, rows_s, lw, dims,
                                 fox_dim ** -0.5, sample_mixer)
        last = l == depth - 1
        xp = _ffn(xp1, ffn_norm_w[l], mods_p[4], mods_p[3], mods_p[5], wg, wu, wd, final_norm_w, last,
                  seq, tm_p)
        xs = _ffn(xs1, ffn_norm_w[l], mods_s[4], mods_s[3], mods_s[5], wg, wu, wd, final_norm_w, last,
                  dec_seq, rows_s)
        new_p.append(st_p)
        new_s.append(st_s)

    y_prompt = xp.reshape(batch, seq, d)
    y_sample = xs.reshape(dec_batch, dec_seq, d)
    kv_p = lambda t: t.reshape(batch, seq, fox_heads, fox_dim)
    kv_s = lambda t: t.reshape(dec_batch, dec_seq, fox_heads, fox_dim)
    k_prompt = jnp.stack([kv_p(s[0]) for s in new_p])
    v_prompt = jnp.stack([kv_p(s[1]) for s in new_p])
    logf_prompt = jnp.stack([s[2].reshape(batch, seq, fox_heads) for s in new_p])
    ret_prompt = jnp.stack([s[3] for s in new_p])
    k_sample = jnp.stack([kv_s(s[0]) for s in new_s])
    v_sample = jnp.stack([kv_s(s[1]) for s in new_s])
    logf_sample = jnp.stack([s[2].reshape(dec_batch, dec_seq, fox_heads) for s in new_s])
    ret_sample = jnp.stack([s[3] for s in new_s])
    return (y_prompt, y_sample, k_prompt, v_prompt, logf_prompt, ret_prompt, k_sample, v_sample,
            logf_sample, ret_sample)
```

```python
import functools

import jax
import jax.numpy as jnp
import numpy as np
from jax import lax
from jax.experimental import pallas as pl
from jax.experimental.pallas import tpu as pltpu

F32 = jnp.float32
BF16 = jnp.bfloat16

NORM_EPS = 1e-6
GN_EPS = 1e-5
ROPE_BASE = 10000.0
NEG = -1e30
RET_CHUNK = 128
LANES = 128
MIB = 1 << 20


def _params(semantics, vmem_mib):
    return pltpu.CompilerParams(dimension_semantics=semantics, vmem_limit_bytes=vmem_mib * MIB)


def _nt_dot(a, b):
    return lax.dot_general(a, b, (((1,), (1,)), ((), ())), preferred_element_type=F32)


def _tn_dot(a, b):
    return lax.dot_general(a, b, (((0,), (0,)), ((), ())), preferred_element_type=F32)


def _dot(a, b):
    return jnp.dot(a, b, preferred_element_type=F32)


def _sigmoid(x):
    return 1.0 / (1.0 + jnp.exp(-x))


def _silu(x):
    return x * _sigmoid(x)


def _rms(x, w):
    return x * lax.rsqrt(jnp.mean(x * x, axis=-1, keepdims=True) + NORM_EPS) * w


def _split_bf16(x):
    hi = x.astype(BF16)
    lo = (x - hi.astype(F32)).astype(BF16)
    return hi, lo


def _row_spec(tm, width):
    return pl.BlockSpec((tm, width), lambda i: (i, 0))


def _const_spec(shape):
    return pl.BlockSpec(shape, lambda *_: (0,) * len(shape))


def _mod_spec(arr, tm, rows_per_batch):
    if arr.ndim == 3:
        tiles = rows_per_batch // tm
        return pl.BlockSpec((None, 1, arr.shape[-1]), lambda i: (i // tiles, 0, 0))
    return _row_spec(tm, arr.shape[-1])


def _ada_kernel(c_ref, w_ref, b_ref, o_ref):
    a = _silu(c_ref[...]).astype(BF16)
    o_ref[...] = _dot(a, w_ref[...].astype(BF16)) + b_ref[...]


def _ada(c, w, b, tn=1536):
    rows, d = c.shape
    n = w.shape[1]
    return pl.pallas_call(
        _ada_kernel,
        grid=(n // tn,),
        in_specs=[_const_spec((rows, d)), pl.BlockSpec((d, tn), lambda j: (0, j)),
                  pl.BlockSpec((1, tn), lambda j: (0, j))],
        out_specs=pl.BlockSpec((rows, tn), lambda j: (0, j)),
        out_shape=jax.ShapeDtypeStruct((rows, n), F32),
        compiler_params=_params(("parallel",), 40),
        name="ada",
    )(c, w, b.reshape(1, n))


def _norm_mod_kernel(x_ref, nw_ref, sc_ref, sh_ref, o_ref):
    y = _rms(x_ref[...], nw_ref[...])
    o_ref[...] = (y * (1.0 + sc_ref[...]) + sh_ref[...]).astype(o_ref.dtype)


def _norm_mod(x, nw, sc, sh, rows_per_batch, tm):
    rows, d = x.shape
    return pl.pallas_call(
        _norm_mod_kernel,
        grid=(rows // tm,),
        in_specs=[_row_spec(tm, d), _const_spec((1, d)), _mod_spec(sc, tm, rows_per_batch),
                  _mod_spec(sh, tm, rows_per_batch)],
        out_specs=_row_spec(tm, d),
        out_shape=jax.ShapeDtypeStruct((rows, d), BF16),
        compiler_params=_params(("parallel",), 32),
        name="norm_mod",
    )(x, nw.reshape(1, d), sc, sh)


def _proj_kernel(h_ref, w_ref, *o_refs, scale):
    z = _dot(h_ref[...], w_ref[...])
    if scale != 1.0:
        z = z * scale
    for o_ref in o_refs:
        o_ref[...] = z.astype(o_ref.dtype)


def _proj(h, w, dtypes, tm, scale=1.0):
    rows, d = h.shape
    n = w.shape[1]
    outs = pl.pallas_call(
        functools.partial(_proj_kernel, scale=scale),
        grid=(rows // tm,),
        in_specs=[_row_spec(tm, d), _const_spec((d, n))],
        out_specs=[_row_spec(tm, n) for _ in dtypes],
        out_shape=[jax.ShapeDtypeStruct((rows, n), dt) for dt in dtypes],
        compiler_params=_params(("parallel",), 40),
        name="proj",
    )(h, w)
    return outs


def _proj_kv_kernel(h_ref, w_ref, ot_ref, o16_ref):
    z = _dot(h_ref[...], w_ref[...])
    o16_ref[...] = z.astype(o16_ref.dtype)
    ot_ref[...] = z.T


def _proj_kv(h, w, batch, tm):
    rows, d = h.shape
    n = w.shape[1]
    seq = rows // batch
    tiles = seq // tm
    return pl.pallas_call(
        _proj_kv_kernel,
        grid=(rows // tm,),
        in_specs=[_row_spec(tm, d), _const_spec((d, n))],
        out_specs=[pl.BlockSpec((None, n, tm), lambda i: (i // tiles, 0, i % tiles)), _row_spec(tm, n)],
        out_shape=[jax.ShapeDtypeStruct((batch, n, seq), F32), jax.ShapeDtypeStruct((rows, n), BF16)],
        compiler_params=_params(("parallel",), 40),
        name="proj_kv",
    )(h, w)


def _proj_rope_kernel(h_ref, w_ref, cos_ref, sin_ref, o_ref, *, scale, head_dim):
    z = _dot(h_ref[...], w_ref[...])
    cos = cos_ref[...]
    sin = sin_ref[...]
    for hd in range(z.shape[1] // head_dim):
        zh = z[:, hd * head_dim:(hd + 1) * head_dim]
        rot = zh * cos + pltpu.roll(zh, head_dim // 2, 1) * sin
        o_ref[:, hd * head_dim:(hd + 1) * head_dim] = (rot * scale).astype(o_ref.dtype)


def _proj_rope(h, w, cos2, sin2, pos_tiles, tm, scale, head_dim):
    rows, d = h.shape
    n = w.shape[1]
    tab = pl.BlockSpec((tm, head_dim), lambda i: (i % pos_tiles, 0))
    return pl.pallas_call(
        functools.partial(_proj_rope_kernel, scale=scale, head_dim=head_dim),
        grid=(rows // tm,),
        in_specs=[_row_spec(tm, d), _const_spec((d, n)), tab, tab],
        out_specs=_row_spec(tm, n),
        out_shape=jax.ShapeDtypeStruct((rows, n), BF16),
        compiler_params=_params(("parallel",), 40),
        name="proj_rope",
    )(h, w, cos2, sin2)


def _log_sigmoid(x):
    return jnp.minimum(x, 0.0) - jnp.log1p(jnp.exp(-jnp.abs(x)))


def _proj_logf_kernel(h_ref, w_ref, b_ref, o_ref):
    z = _dot(h_ref[...], w_ref[...]) + b_ref[...]
    o_ref[...] = _log_sigmoid(z)[:, :o_ref.shape[1]]


def _proj_logf(h, w_pad, b_pad, heads, tm):
    rows, d = h.shape
    n = w_pad.shape[1]
    return pl.pallas_call(
        _proj_logf_kernel,
        grid=(rows // tm,),
        in_specs=[_row_spec(tm, d), _const_spec((d, n)), _const_spec((1, n))],
        out_specs=_row_spec(tm, heads),
        out_shape=jax.ShapeDtypeStruct((rows, heads), F32),
        compiler_params=_params(("parallel",), 32),
        name="proj_logf",
    )(h, w_pad, b_pad)


def _cumsum_kernel(lf_ref, tri_ref, o_ref, carry_ref):
    @pl.when(pl.program_id(1) == 0)
    def _():
        carry_ref[...] = jnp.zeros_like(carry_ref)

    hi, lo = _split_bf16(lf_ref[...])
    tri = tri_ref[...]
    f = _dot(tri, hi) + _dot(tri, lo) + carry_ref[...]
    o_ref[...] = f
    carry_ref[...] = f[f.shape[0] - 1:, :]


def _cumsum_seq(lf, batch, tc=512):
    rows, heads = lf.shape
    nt = rows // batch // tc
    tri = (jnp.arange(tc)[:, None] >= jnp.arange(tc)[None, :]).astype(BF16)
    return pl.pallas_call(
        _cumsum_kernel,
        grid=(batch, nt),
        in_specs=[pl.BlockSpec((tc, heads), lambda b, t: (b * nt + t, 0)), _const_spec((tc, tc))],
        out_specs=pl.BlockSpec((tc, heads), lambda b, t: (b * nt + t, 0)),
        out_shape=jax.ShapeDtypeStruct((rows, heads), F32),
        scratch_shapes=[pltpu.VMEM((1, heads), F32)],
        compiler_params=_params(("parallel", "arbitrary"), 32),
        name="cumsum_logf",
    )(lf, tri)


LOG2E = 1.4426950408889634
_BIAS_PARTS = 3
_M_INIT = -1e28


def _split3(x):
    hi = x.astype(BF16).astype(F32)
    rest = x - hi
    mid = rest.astype(BF16).astype(F32)
    lo = (rest - mid).astype(BF16).astype(F32)
    return hi, mid, lo


def _fox_prompt_kernel(q_ref, k_ref, v_ref, f_ref, o_ref, kaug_ref, vt_ref, qt_ref, m_ref, acc_ref,
                       s_ref, p_ref, alpha_ref, smax_ref, *, tq, tk, head_dim):
    qi = pl.program_id(2)
    nk = kaug_ref.shape[1]
    base = [head_dim * (1 - hd) for hd in range(2)]

    def own(lane, hd):
        return (lane >= hd * head_dim) & (lane < (hd + 1) * head_dim)

    def place(lane, parts_a, parts_b, hd):
        out = jnp.zeros(lane.shape, F32)
        for i, part in enumerate(list(parts_a) + list(parts_b)):
            out = jnp.where(lane == base[hd] + i, part, out)
        return out

    @pl.when(qi == 0)
    def _():
        ones = (1.0,) * _BIAS_PARTS
        lane = lax.broadcasted_iota(jnp.int32, (tk, LANES), 1)

        def prep(c, carry):
            start = pl.multiple_of(c * tk, tk)
            kc = k_ref[pl.ds(start, tk), :]
            vc = v_ref[pl.ds(start, tk), :].astype(F32)
            fc = f_ref[pl.ds(start, tk), :] * (-LOG2E)
            for hd in range(2):
                aug = place(lane, _split3(fc[:, hd:hd + 1]), ones, hd)
                kaug_ref[hd, c] = jnp.where(own(lane, hd), kc, aug.astype(BF16))
                v_aug = jnp.where(own(lane, hd), vc, jnp.where(lane == base[hd], 1.0, 0.0))
                vt_ref[hd, c] = v_aug.T.astype(BF16)
            return carry

        lax.fori_loop(0, nk, prep, 0)

    lane_q = lax.broadcasted_iota(jnp.int32, (tq, LANES), 1)
    q = q_ref[...].astype(F32)
    f_first = f_ref[pl.ds(pl.multiple_of(qi * tq, tq), 1), :] * LOG2E
    for hd in range(2):
        aug = place(lane_q, (1.0,) * _BIAS_PARTS, _split3(f_first[:, hd:hd + 1]), hd)
        qt_ref[hd] = jnp.where(own(lane_q, hd), q, aug).T.astype(BF16)
    m_ref[...] = jnp.full_like(m_ref, _M_INIT)
    acc_ref[...] = jnp.zeros_like(acc_ref)
    s_ref[...] = jnp.full_like(s_ref, NEG)
    smax_ref[...] = jnp.full_like(smax_ref, NEG)
    p_ref[...] = jnp.zeros_like(p_ref)
    alpha_ref[...] = jnp.ones_like(alpha_ref)

    def scores(t, diagonal):
        for hd in range(2):
            s_t = _dot(kaug_ref[hd, t], qt_ref[hd])
            if diagonal:
                key = lax.broadcasted_iota(jnp.int32, (tk, tq), 0)
                qry = lax.broadcasted_iota(jnp.int32, (tk, tq), 1)
                s_t = jnp.where(key <= qry, s_t, NEG)
            s_ref[hd] = s_t
            smax_ref[hd] = jnp.max(s_t, axis=0, keepdims=True)

    def softmax():
        for hd in range(2):
            m_prev = m_ref[hd]
            m_new = jnp.maximum(m_prev, smax_ref[hd])
            alpha_ref[hd] = jnp.exp2(m_prev - m_new)
            p_ref[hd] = jnp.exp2(s_ref[hd] - m_new).astype(BF16)
            m_ref[hd] = m_new

    def accumulate(t):
        j = jnp.clip(t, 0, nk - 1)
        for hd in range(2):
            acc_ref[hd] = alpha_ref[hd] * acc_ref[hd] + _dot(vt_ref[hd, j], p_ref[hd])

    def body(t, carry):
        accumulate(t - 2)
        softmax()
        scores(t, False)
        return carry

    lax.fori_loop(0, qi, body, 0)
    accumulate(qi - 2)
    softmax()
    scores(qi, True)
    accumulate(qi - 1)
    softmax()
    accumulate(qi)

    outs = []
    for hd in range(2):
        acc = acc_ref[hd]
        outs.append((acc / acc[base[hd]:base[hd] + 1, :]).T)
    o_ref[...] = jnp.where(own(lane_q, 0), outs[0], outs[1]).astype(o_ref.dtype)


def _fox_prompt(q, k, v, f_cum, batch, seq, heads, head_dim, tq=512, tk=512):
    assert 2 * head_dim == LANES and 2 * _BIAS_PARTS <= head_dim and tq == tk
    pairs = heads // 2
    nq = seq // tq
    nk = seq // tk
    width = heads * head_dim
    q3, k3, v3 = (t.reshape(batch, seq, width) for t in (q, k, v))
    f_pairs = f_cum.reshape(batch, seq, pairs, 2).transpose(0, 2, 1, 3)
    out = pl.pallas_call(
        functools.partial(_fox_prompt_kernel, tq=tq, tk=tk, head_dim=head_dim),
        grid=(batch, pairs, nq),
        in_specs=[
            pl.BlockSpec((None, tq, LANES), lambda b, p, i: (b, i, p)),
            pl.BlockSpec((None, seq, LANES), lambda b, p, i: (b, 0, p)),
            pl.BlockSpec((None, seq, LANES), lambda b, p, i: (b, 0, p)),
            pl.BlockSpec((None, None, seq, 2), lambda b, p, i: (b, p, 0, 0)),
        ],
        out_specs=pl.BlockSpec((None, tq, LANES), lambda b, p, i: (b, i, p)),
        out_shape=jax.ShapeDtypeStruct((batch, seq, width), BF16),
        scratch_shapes=[pltpu.VMEM((2, nk, tk, LANES), BF16), pltpu.VMEM((2, nk, LANES, tk), BF16),
                        pltpu.VMEM((2, LANES, tq), BF16), pltpu.VMEM((2, 1, tq), F32),
                        pltpu.VMEM((2, LANES, tq), F32), pltpu.VMEM((2, tk, tq), F32),
                        pltpu.VMEM((2, tk, tq), BF16), pltpu.VMEM((2, 1, tq), F32),
                        pltpu.VMEM((2, 1, tq), F32)],
        compiler_params=_params(("parallel", "parallel", "arbitrary"), 32),
        name="fox_prompt",
    )(q3, k3, v3, f_pairs)
    return out.reshape(batch * seq, width)


def _retention_kernel(q_ref, k_ref, v_ref, g_ref, s0_ref, dec_ref, qd_ref, kd_ref, gc_ref, gnw_ref,
                      r_ref, sfin_ref, st_ref, *, heads, dk, dv):
    c = pl.program_id(1)

    @pl.when(c == 0)
    def _():
        st_ref[...] = s0_ref[...]

    for hd in range(heads):
        ksl = slice(hd * dk, (hd + 1) * dk)
        vsl = slice(hd * dv, (hd + 1) * dv)
        q = q_ref[:, ksl]
        k = k_ref[:, ksl]
        v = v_ref[:, vsl]
        st = st_ref[hd]
        inner = _nt_dot(q, k) * dec_ref[hd]
        o = _dot(inner.astype(BF16), v) + _dot(q, st.astype(BF16)) * qd_ref[hd]
        k_dec = (k.astype(F32) * kd_ref[hd]).astype(BF16)
        st_ref[hd] = gc_ref[hd] * st + _tn_dot(k_dec, v)
        mu = jnp.mean(o, axis=-1, keepdims=True)
        cen = o - mu
        var = jnp.mean(cen * cen, axis=-1, keepdims=True)
        r = cen * lax.rsqrt(var + GN_EPS) * gnw_ref[:, vsl] * _silu(g_ref[:, vsl].astype(F32))
        r_ref[:, vsl] = r.astype(r_ref.dtype)

    @pl.when(c == pl.num_programs(1) - 1)
    def _():
        sfin_ref[...] = st_ref[...]


def _retention(q, k, v, g, state0, tables, gn_w, chunk):
    batch, seq, _ = q.shape
    _, heads, dk, dv = state0.shape
    dec, qd, kd, gc = tables
    nc = seq // chunk
    tok = lambda w: pl.BlockSpec((None, chunk, w), lambda b, c: (b, c, 0))
    st_spec = pl.BlockSpec((None, heads, dk, dv), lambda b, c: (b, 0, 0, 0))
    return pl.pallas_call(
        functools.partial(_retention_kernel, heads=heads, dk=dk, dv=dv),
        grid=(batch, nc),
        in_specs=[tok(heads * dk), tok(heads * dk), tok(heads * dv), tok(heads * dv), st_spec,
                  _const_spec(dec.shape), _const_spec(qd.shape), _const_spec(kd.shape),
                  _const_spec(gc.shape), _const_spec((1, heads * dv))],
        out_specs=[tok(heads * dv), st_spec],
        out_shape=[jax.ShapeDtypeStruct((batch, seq, heads * dv), BF16),
                   jax.ShapeDtypeStruct(state0.shape, F32)],
        scratch_shapes=[pltpu.VMEM((heads, dk, dv), F32)],
        compiler_params=_params(("parallel", "arbitrary"), 32),
        name="retention",
    )(q, k, v, g, state0, dec, qd, kd, gc, gn_w.reshape(1, heads * dv))


def _retention_tables(heads, chunk, padded):
    log_gamma = jnp.log(1.0 - 2.0 ** (-5.0 - jnp.arange(heads, dtype=F32)))
    i = jnp.arange(chunk, dtype=F32)
    diff = i[:, None] - i[None, :]
    dec = jnp.where(diff >= 0, jnp.exp(log_gamma[:, None, None] * jnp.maximum(diff, 0.0)), 0.0)
    qd = jnp.exp(log_gamma[:, None] * (i[None, :] + 1.0))[:, :, None]
    kd = jnp.exp(log_gamma[:, None] * (chunk - 1.0 - i[None, :]))[:, :, None]
    gc = jnp.exp(log_gamma * chunk)[:, None, None]
    pad = padded - chunk
    dec = jnp.pad(dec, ((0, 0), (0, pad), (0, pad)))
    qd = jnp.pad(qd, ((0, 0), (0, pad), (0, 0)))
    kd = jnp.pad(kd, ((0, 0), (0, pad), (0, 0)))
    return dec, qd, kd, gc


def _fox_sample_kernel(pt_ref, qbd_ref, kn_ref, vn_ref, lfn_ref, usuf_ref, uinc_ref, *rest,
                       pages_per_step, heads, head_dim, new_tokens, page):
    g_n = pages_per_step
    k_refs = rest[:g_n]
    v_refs = rest[g_n:2 * g_n]
    lf_refs = rest[2 * g_n:3 * g_n]
    o_ref, m_ref, l_ref, acc_ref, carry_ref, bias_ref, kt_ref, vt_ref = rest[3 * g_n:]
    j = pl.program_id(1)
    rows = new_tokens * heads
    qbd = qbd_ref[...]

    def set_bias(g, per_head):
        for t in range(new_tokens):
            bias_ref[t * heads:(t + 1) * heads, g * page:(g + 1) * page] = per_head

    def update(s, weighted_values):
        m_prev = m_ref[...]
        m_new = jnp.maximum(m_prev, jnp.max(s, axis=1, keepdims=True))
        alpha = jnp.exp(m_prev - m_new)
        p = jnp.exp(s - m_new)
        l_ref[...] = alpha * l_ref[...] + jnp.sum(p, axis=1, keepdims=True)
        acc_ref[...] = alpha * acc_ref[...] + weighted_values(p.astype(BF16))
        m_ref[...] = m_new

    @pl.when(j == 0)
    def _():
        m_ref[...] = jnp.full_like(m_ref, NEG)
        l_ref[...] = jnp.zeros_like(l_ref)
        acc_ref[...] = jnp.zeros_like(acc_ref)
        carry_ref[...] = jnp.zeros_like(carry_ref)
        kn = kn_ref[...]
        zpad = jnp.zeros((page - kn.shape[0], kn.shape[1]), kn.dtype)
        kp = jnp.concatenate([kn, zpad], axis=0)
        vp = jnp.concatenate([vn_ref[...], zpad], axis=0)
        lfn = lfn_ref[...]
        lf = jnp.concatenate([lfn, jnp.zeros((page - lfn.shape[0], lfn.shape[1]), F32)], axis=0)
        hi, lo = _split_bf16(lf)
        set_bias(0, (_tn_dot(hi, uinc_ref[...]) + _tn_dot(lo, uinc_ref[...]))[:, :page])
        row = lax.broadcasted_iota(jnp.int32, (rows, page), 0)
        col = lax.broadcasted_iota(jnp.int32, (rows, page), 1)
        s = jnp.where(col * heads <= row, _nt_dot(qbd, kp) + bias_ref[:, :page], NEG)
        update(s, lambda p: _dot(p, vp))

    carry = carry_ref[...]
    for g in range(g_n):
        hi, lo = _split_bf16(lf_refs[g][...])
        b = _dot(hi, usuf_ref[...]) + _dot(lo, usuf_ref[...])
        set_bias(g, b[:, :page] + carry)
        carry = carry + b[:, page:page + 1]
        kt_ref[:, g * page:(g + 1) * page] = k_refs[g][...].reshape(heads * head_dim, page).astype(BF16)
        vt_ref[:, g * page:(g + 1) * page] = v_refs[g][...].reshape(heads * head_dim, page).astype(BF16)
    carry_ref[...] = carry
    s = _dot(qbd, kt_ref[...]) + bias_ref[...]
    update(s, lambda p: _nt_dot(p, vt_ref[...]))

    @pl.when(j == pl.num_programs(1) - 1)
    def _():
        out = acc_ref[...] / l_ref[...]
        row = lax.broadcasted_iota(jnp.int32, out.shape, 0)
        col = lax.broadcasted_iota(jnp.int32, out.shape, 1)
        out = jnp.where((row % heads) == (col // head_dim), out, 0.0)
        o_ref[...] = out.reshape(new_tokens, heads, out.shape[1]).sum(axis=1)


def _fox_sample(q, k_new, v_new, lf_new, cache_kt, cache_vt, cache_lft, page_table, pages_per_step=16):
    batch, new_tokens, width = q.shape
    _, heads, head_dim, page = cache_kt.shape
    n_pages = page_table.shape[1]
    rows = new_tokens * heads
    g_n = pages_per_step
    assert n_pages % g_n == 0 and page == LANES and width == heads * head_dim

    eye = jnp.eye(heads, dtype=q.dtype)
    qbd = (q.reshape(batch, new_tokens, heads, 1, head_dim) * eye[None, None, :, :, None]
           ).reshape(batch, rows, width)
    pad_t = 16 - new_tokens
    k_new = jnp.pad(k_new, ((0, 0), (0, pad_t), (0, 0)))
    v_new = jnp.pad(v_new, ((0, 0), (0, pad_t), (0, 0)))
    lf_new = jnp.pad(lf_new, ((0, 0), (0, 8 - new_tokens), (0, 0)))

    idx = jnp.arange(page)
    ones_col = (jnp.arange(page)[None, :] == 0).astype(F32) * jnp.ones((page, 1), F32)
    usuf = jnp.concatenate([(idx[:, None] > idx[None, :]).astype(F32), ones_col], axis=1).astype(BF16)
    uinc = jnp.concatenate([-(idx[:, None] <= idx[None, :]).astype(F32), ones_col], axis=1).astype(BF16)

    def page_map(g):
        return lambda b, j, pt: (pt[b, n_pages - 1 - (j * g_n + g)], 0, 0, 0)

    per_b = lambda shape: pl.BlockSpec((None,) + shape, lambda b, j, pt: (b, 0, 0))
    in_specs = [per_b((rows, width)), per_b((16, width)), per_b((16, width)), per_b((8, heads)),
                pl.BlockSpec(usuf.shape, lambda b, j, pt: (0, 0)),
                pl.BlockSpec(uinc.shape, lambda b, j, pt: (0, 0))]
    in_specs += [pl.BlockSpec((None, heads, head_dim, page), page_map(g)) for g in range(g_n)]
    in_specs += [pl.BlockSpec((None, heads, head_dim, page), page_map(g)) for g in range(g_n)]
    in_specs += [pl.BlockSpec((None, heads, page), lambda b, j, pt, g=g: page_map(g)(b, j, pt)[:3])
                 for g in range(g_n)]
    grid_spec = pltpu.PrefetchScalarGridSpec(
        num_scalar_prefetch=1,
        grid=(batch, n_pages // g_n),
        in_specs=in_specs,
        out_specs=pl.BlockSpec((None, new_tokens, width), lambda b, j, pt: (b, 0, 0)),
        scratch_shapes=[pltpu.VMEM((rows, 1), F32), pltpu.VMEM((rows, 1), F32),
                        pltpu.VMEM((rows, width), F32), pltpu.VMEM((heads, 1), F32),
                        pltpu.VMEM((rows, g_n * page), F32), pltpu.VMEM((width, g_n * page), BF16),
                        pltpu.VMEM((width, g_n * page), BF16)],
    )
    return pl.pallas_call(
        functools.partial(_fox_sample_kernel, pages_per_step=g_n, heads=heads, head_dim=head_dim,
                          new_tokens=new_tokens, page=page),
        grid_spec=grid_spec,
        out_shape=jax.ShapeDtypeStruct((batch, new_tokens, width), F32),
        compiler_params=_params(("parallel", "arbitrary"), 56),
        name="fox_sample",
    )(page_table, qbd, k_new, v_new, lf_new, usuf, uinc,
      *([cache_kt] * g_n), *([cache_vt] * g_n), *([cache_lft] * g_n))


def _mix_out_kernel(a_ref, r_ref, ga_ref, gb_ref, x_ref, g1_ref, wo_ref, o_ref):
    m = (_sigmoid(ga_ref[...].astype(F32)) * a_ref[...].astype(F32)
         + _sigmoid(gb_ref[...].astype(F32)) * r_ref[...].astype(F32))
    o_ref[...] = x_ref[...] + g1_ref[...] * _dot(m.astype(BF16), wo_ref[...])


def _mix_out(a, r, ga, gb, x, g1, wo, rows_per_batch, tm):
    rows, d = x.shape
    w = a.shape[1]
    return pl.pallas_call(
        _mix_out_kernel,
        grid=(rows // tm,),
        in_specs=[_row_spec(tm, w), _row_spec(tm, w), _row_spec(tm, w), _row_spec(tm, w),
                  _row_spec(tm, d), _mod_spec(g1, tm, rows_per_batch), _const_spec(wo.shape)],
        out_specs=_row_spec(tm, d),
        out_shape=jax.ShapeDtypeStruct((rows, d), F32),
        compiler_params=_params(("parallel",), 48),
        name="mix_out",
    )(a, r, ga, gb, x, g1, wo)


def _ffn_kernel(x_ref, nw_ref, sc_ref, sh_ref, g2_ref, wg_ref, wu_ref, wd_ref, fw_ref, o_ref, acc_ref,
                *, chunk, final_norm):
    x = x_ref[...]
    h = (_rms(x, nw_ref[...]) * (1.0 + sc_ref[...]) + sh_ref[...]).astype(BF16)
    d_ff = wg_ref.shape[1]
    for c in range(d_ff // chunk):
        sl = slice(c * chunk, (c + 1) * chunk)
        gate = _dot(h, wg_ref[:, sl])
        up = _dot(h, wu_ref[:, sl])
        part = _dot((_silu(gate) * up).astype(BF16), wd_ref[sl, :])
        if c == 0:
            acc_ref[...] = part
        else:
            acc_ref[...] += part
    x2 = x + g2_ref[...] * acc_ref[...]
    o_ref[...] = _rms(x2, fw_ref[...]) if final_norm else x2


def _ffn(x, nw, sc, sh, g2, wg, wu, wd, fw, final_norm, rows_per_batch, tm, chunk=256):
    rows, d = x.shape
    d_ff = wg.shape[1]
    assert d_ff % chunk == 0
    once = lambda shape: pl.BlockSpec(shape, lambda *_: (0,) * len(shape), pipeline_mode=pl.Buffered(1))
    return pl.pallas_call(
        functools.partial(_ffn_kernel, chunk=chunk, final_norm=final_norm),
        grid=(rows // tm,),
        in_specs=[_row_spec(tm, d), _const_spec((1, d)), _mod_spec(sc, tm, rows_per_batch),
                  _mod_spec(sh, tm, rows_per_batch), _mod_spec(g2, tm, rows_per_batch),
                  once((d, d_ff)), once((d, d_ff)), once((d_ff, d)), _const_spec((1, d))],
        out_specs=_row_spec(tm, d),
        out_shape=jax.ShapeDtypeStruct((rows, d), F32),
        scratch_shapes=[pltpu.VMEM((tm, d), F32)],
        compiler_params=_params(("parallel",), 52),
        name="ffn",
    )(x, nw.reshape(1, d), sc, sh, g2, wg, wu, wd, fw.reshape(1, d))


def _rope_tables(pos, head_dim):
    half = head_dim // 2
    inv = ROPE_BASE ** (-jnp.arange(half, dtype=F32) / half)
    ang = pos.astype(F32)[:, None] * inv[None, :]
    cos, sin = jnp.cos(ang), jnp.sin(ang)
    return jnp.concatenate([cos, cos], axis=-1), jnp.concatenate([-sin, sin], axis=-1)


def _group_layer(x, rows_per_batch, mods, rope_tabs, pos_tiles, tm, tm_proj, lw, dims, fq_scale, mixer,
                 kv_batch=None):
    sh1, sc1, g1, sh2, sc2, g2 = mods
    fox_heads, fox_dim, ret_heads, ret_dk, ret_dv = dims
    cos2, sin2 = rope_tabs
    h = _norm_mod(x, lw["attn_norm_w"], sc1, sh1, rows_per_batch, tm)
    (fq,) = _proj(h, lw["w_fq"], (BF16,), tm_proj, scale=fq_scale)
    if kv_batch is None:
        fk32, fk16 = _proj(h, lw["w_fk"], (F32, BF16), tm_proj)
        fv32, fv16 = _proj(h, lw["w_fv"], (F32, BF16), tm_proj)
    else:
        fk32, fk16 = _proj_kv(h, lw["w_fk"], kv_batch, tm_proj)
        fv32, fv16 = _proj_kv(h, lw["w_fv"], kv_batch, tm_proj)
    logf = _proj_logf(h, lw["w_f"], lw["b_f"], fox_heads, tm_proj)
    rq = _proj_rope(h, lw["w_rq"], cos2, sin2, pos_tiles, tm_proj, ret_dk ** -0.5, ret_dk)
    rk = _proj_rope(h, lw["w_rk"], cos2, sin2, pos_tiles, tm_proj, 1.0, ret_dk)
    (rv,) = _proj(h, lw["w_rv"], (BF16,), tm_proj)
    (rg,) = _proj(h, lw["w_rg"], (F32,), tm_proj)
    (ga,) = _proj(h, lw["w_ga"], (F32,), tm_proj)
    (gb,) = _proj(h, lw["w_gb"], (F32,), tm_proj)
    a, r, s_fin = mixer(fq, fk16, fv16, logf, rq, rk, rv, rg)
    x1 = _mix_out(a, r, ga, gb, x, g1, lw["w_o"], rows_per_batch, tm)
    return x1, (fk32, fv32, logf, s_fin)


def kernel(x_prompt, x_sample, cache_k, cache_v, cache_logf, state_ret, page_table, c_prompt, c_sample,
           w_ada, b_ada, attn_norm_w, w_in, b_f, ret_gn_w, w_o, ffn_norm_w, w_gate, w_up, w_down,
           final_norm_w):
    batch, seq, d = x_prompt.shape
    dec_batch, dec_seq, _ = x_sample.shape
    depth, n_phys, page, fox_heads, fox_dim = cache_k.shape
    _, _, ret_heads, ret_dk, ret_dv = state_ret.shape
    n_pages = page_table.shape[1]
    past_len = n_pages * page
    w_fox = fox_heads * fox_dim
    w_rk = ret_heads * ret_dk
    w_rv = ret_heads * ret_dv
    dims = (fox_heads, fox_dim, ret_heads, ret_dk, ret_dv)
    splits = (w_fox, w_fox, w_fox, fox_heads, w_rk, w_rk, w_rv, w_rv, d, d)
    offs = np.concatenate([[0], np.cumsum(splits)])
    seg_names = ("w_fq", "w_fk", "w_fv", "w_f", "w_rq", "w_rk", "w_rv", "w_rg", "w_ga", "w_gb")

    xp = x_prompt.reshape(batch * seq, d)
    xs = x_sample.reshape(dec_batch * dec_seq, d)
    rows_s = dec_batch * dec_seq

    tm_p, tm_proj_p = 512, 1024
    rope_p = _rope_tables(jnp.arange(seq, dtype=jnp.int32), ret_dk)
    pos_s = past_len + jnp.arange(dec_seq, dtype=jnp.int32)
    rope_s = tuple(jnp.tile(t, (dec_batch, 1)) for t in _rope_tables(pos_s, ret_dk))
    tab_p = _retention_tables(ret_heads, RET_CHUNK, RET_CHUNK)
    dec_pad = 16
    tab_s = _retention_tables(ret_heads, dec_seq, dec_pad)

    ck = jnp.transpose(cache_k, (0, 1, 3, 4, 2)).reshape(depth * n_phys, fox_heads, fox_dim, page)
    cv = jnp.transpose(cache_v, (0, 1, 3, 4, 2)).reshape(depth * n_phys, fox_heads, fox_dim, page)
    clf = jnp.transpose(cache_logf, (0, 1, 3, 2)).reshape(depth * n_phys, fox_heads, page)

    c_all = jnp.concatenate([c_prompt, c_sample], axis=0)
    c_rows = -(-c_all.shape[0] // 16) * 16
    c_all = jnp.pad(c_all, ((0, c_rows - c_all.shape[0]), (0, 0)))

    new_p, new_s = [], []
    for l in range(depth):
        mod = _ada(c_all, w_ada[l], b_ada[l])
        mods_p = tuple(m.reshape(batch, 1, d) for m in jnp.split(mod[:batch], 6, axis=-1))
        mods_s = tuple(jnp.repeat(m, dec_seq, axis=0)
                       for m in jnp.split(mod[batch:batch + dec_batch], 6, axis=-1))
        lw = {n: w_in[l][:, offs[i]:offs[i + 1]].astype(BF16) for i, n in enumerate(seg_names)}
        lw["w_f"] = jnp.pad(lw["w_f"], ((0, 0), (0, LANES - fox_heads)))
        lw["b_f"] = jnp.pad(b_f[l].astype(F32), (0, LANES - fox_heads)).reshape(1, LANES)
        lw["attn_norm_w"] = attn_norm_w[l]
        lw["w_o"] = w_o[l].astype(BF16)
        wg, wu, wd = w_gate[l].astype(BF16), w_up[l].astype(BF16), w_down[l].astype(BF16)

        def prompt_mixer(fq, fk, fv, logf, rq, rk, rv, rg):
            f_cum = _cumsum_seq(logf, batch)
            a = _fox_prompt(fq, fk, fv, f_cum, batch, seq, fox_heads, fox_dim)
            to3 = lambda t: t.reshape(batch, seq, t.shape[-1])
            s0 = jnp.zeros((batch, ret_heads, ret_dk, ret_dv), F32)
            r, s_fin = _retention(to3(rq), to3(rk), to3(rv), to3(rg), s0, tab_p, ret_gn_w[l], RET_CHUNK)
            return a, r.reshape(batch * seq, w_rv), s_fin

        def sample_mixer(fq, fk, fv, logf, rq, rk, rv, rg):
            to3 = lambda t: t.reshape(dec_batch, dec_seq, t.shape[-1])
            page_ids = page_table + l * n_phys
            a = _fox_sample(to3(fq), to3(fk), to3(fv), to3(logf), ck, cv, clf, page_ids)
            padt = lambda t: jnp.pad(to3(t), ((0, 0), (0, dec_pad - dec_seq), (0, 0)))
            r, s_new = _retention(padt(rq), padt(rk), padt(rv), padt(rg), state_ret[l], tab_s,
                                  ret_gn_w[l], dec_pad)
            return a.reshape(rows_s, w_fox), r[:, :dec_seq].reshape(rows_s, w_rv), s_new

        xp1, st_p = _group_layer(xp, seq, mods_p, rope_p, seq // tm_proj_p, tm_p, tm_proj_p, lw, dims,
                                 fox_dim ** -0.5 * LOG2E, prompt_mixer, kv_batch=batch)
        xs1, st_s = _group_layer(xs, dec_seq, mods_s, rope_s, 1, rows_s, rows_s, lw, dims,
                                 fox_dim ** -0.5, sample_mixer)
        last = l == depth - 1
        xp = _ffn(xp1, ffn_norm_w[l], mods_p[4], mods_p[3], mods_p[5], wg, wu, wd, final_norm_w, last,
                  seq, tm_p)
        xs = _ffn(xs1, ffn_norm_w[l], mods_s[4], mods_s[3], mods_s[5], wg, wu, wd, final_norm_w, last,
                  dec_seq, rows_s)
        new_p.append(st_p)
        new_s.append(st_s)

    y_prompt = xp.reshape(batch, seq, d)
    y_sample = xs.reshape(dec_batch, dec_seq, d)
    kv_p = lambda t: t.reshape(batch, fox_heads, fox_dim, seq).transpose(0, 3, 1, 2)
    kv_s = lambda t: t.reshape(dec_batch, dec_seq, fox_heads, fox_dim)
    k_prompt = jnp.stack([kv_p(s[0]) for s in new_p])
    v_prompt = jnp.stack([kv_p(s[1]) for s in new_p])
    logf_prompt = jnp.stack([s[2].reshape(batch, seq, fox_heads) for s in new_p])
    ret_prompt = jnp.stack([s[3] for s in new_p])
    k_sample = jnp.stack([kv_s(s[0]) for s in new_s])
    v_sample = jnp.stack([kv_s(s[1]) for s in new_s])
    logf_sample = jnp.stack([s[2].reshape(dec_batch, dec_seq, fox_heads) for s in new_s])
    ret_sample = jnp.stack([s[3] for s in new_s])
    return (y_prompt, y_sample, k_prompt, v_prompt, logf_prompt, ret_prompt, k_sample, v_sample,
            logf_sample, ret_sample)
```
